```python
import jax, jax.numpy as jnp
from jax import lax
import numpy as np

D_MODEL = 1024
BATCH = 8
SEQ = 2048
DEPTH = 4
DEC_BATCH = 128
DEC_SEQ = 4
PAST_LEN = 2048
PAGE_SIZE = 128

RW_HEADS = 8
RW_HD = 64
RW_W = RW_HEADS * RW_HD
LORA_W = 64
LORA_A = 64
RW_SHIFT = 3 * RW_W + LORA_W + LORA_A
LNX_EPS = 64e-5
POOL_WINDOWS = (2, 4, 8, 16)
POOL_GROUPS = 4
POOL_GW = 128
POOL_W = POOL_GROUPS * POOL_GW
POOL_BUF = 15
FOX_HEADS = 8
FOX_HD = 64
FOX_W = FOX_HEADS * FOX_HD
Q_BLOCK = 128
FORGET_BIAS_LO = 4.0
FORGET_BIAS_HI = 9.0
N_BRANCH = 3
N_IN = RW_SHIFT + RW_W + 2 * POOL_W + 4 * FOX_W + FOX_HEADS + N_BRANCH * D_MODEL
EPS = 1e-6

kernel_name = "rwkv7_pool_fox_gated_hybrid_step"


def _split_points():
    sizes = (RW_SHIFT, RW_W, POOL_W, POOL_W, 3 * FOX_W, FOX_HEADS, FOX_W)
    pts, acc = [], 0
    for s in sizes:
        acc += s
        pts.append(acc)
    return pts


def rms_norm(x, g):
    xf = x.astype(jnp.float32)
    y = xf * lax.rsqrt(jnp.mean(xf * xf, axis=-1, keepdims=True) + EPS) * g.astype(jnp.float32)
    return y.astype(x.dtype)


def rwkv_mix(z, z_last, S0, mu, w0, w2, a0, a2, k_k, k_a, r_k, ln_w, ln_b):
    f32 = jnp.float32
    B, T, _ = z.shape
    z = z.astype(f32)
    z_prev = jnp.concatenate([z_last[:, None].astype(f32), z[:, :-1]], axis=1)
    zs = z + (z_prev - z) * mu
    r, k, v, wl, al = jnp.split(zs, [RW_W, 2 * RW_W, 3 * RW_W, 3 * RW_W + LORA_W], axis=-1)
    w = -jax.nn.softplus(-(w0 + jnp.tanh(wl) @ w2)) - 0.5
    decay = jnp.exp(-jnp.exp(w))
    a = jax.nn.sigmoid(a0 + al @ a2)
    kk = k * k_k
    k = k * (1.0 + (a - 1.0) * k_a)
    heads = lambda t: t.reshape(B, T, RW_HEADS, RW_HD)
    r, k, v, a, kk, decay = heads(r), heads(k), heads(v), heads(a), heads(kk), heads(decay)
    kk = kk / jnp.maximum(jnp.sqrt(jnp.sum(kk * kk, axis=-1, keepdims=True)), 1e-12)

    def step(S, inp):
        r_t, k_t, v_t, d_t, kk_t, a_t = inp
        sa = jnp.einsum('bhvk,bhk->bhv', S, -kk_t)
        S = (S * d_t[:, :, None, :] + sa[..., None] * (kk_t * a_t)[:, :, None, :]
             + v_t[..., None] * k_t[:, :, None, :])
        return S, jnp.einsum('bhvk,bhk->bhv', S, r_t)

    xs = tuple(jnp.moveaxis(t, 1, 0) for t in (r, k, v, decay, kk, a))
    S_T, y = lax.scan(step, S0.astype(f32), xs)
    y = jnp.moveaxis(y, 0, 1)
    mean = jnp.mean(y, axis=-1, keepdims=True)
    var = jnp.mean(jnp.square(y - mean), axis=-1, keepdims=True)
    y = ((y - mean) * lax.rsqrt(var + LNX_EPS)).reshape(B, T, RW_W) * ln_w + ln_b
    bonus = jnp.sum(r * k * r_k.reshape(RW_HEADS, RW_HD), axis=-1, keepdims=True) * v
    return y + bonus.reshape(B, T, RW_W), S_T


def pool_mix(u, prefix, pos0, pool_w, pool_scale):
    f32 = jnp.float32
    B, T, C = u.shape
    ext = jnp.concatenate([prefix.astype(f32), u.astype(f32)], axis=1)
    cs = jnp.concatenate([jnp.zeros((B, 1, C), f32), jnp.cumsum(ext, axis=1)], axis=1)
    pos = pos0 + jnp.arange(T)
    means = []
    for g, w in enumerate(POOL_WINDOWS):
        sl = slice(g * POOL_GW, (g + 1) * POOL_GW)
        wsum = cs[:, POOL_BUF + 1:POOL_BUF + 1 + T, sl] - cs[:, POOL_BUF + 1 - w:POOL_BUF + 1 - w + T, sl]
        cnt = jnp.minimum(pos + 1, w).astype(f32)[None, :, None]
        means.append(wsum / cnt)
    d = (jnp.concatenate(means, axis=-1) - ext[:, POOL_BUF:]).reshape(B, T, POOL_GROUPS, POOL_GW)
    out = jnp.einsum('btgc,gcd->btgd', d, pool_w.astype(f32)).reshape(B, T, POOL_W) * pool_scale
    return out, ext[:, -POOL_BUF:].astype(u.dtype)


def fox_prompt(q, k, v, lf):
    f32 = jnp.float32
    B, S, H, Dh = q.shape
    scale = FOX_HD ** -0.5
    c = jnp.cumsum(lf.astype(f32), axis=1)
    cT = jnp.transpose(c, (0, 2, 1))
    nb = S // Q_BLOCK
    qb = jnp.transpose(q.reshape(B, nb, Q_BLOCK, H, Dh), (1, 0, 2, 3, 4))
    cb = jnp.transpose(c.reshape(B, nb, Q_BLOCK, H), (1, 0, 3, 2))
    kpos = jnp.arange(S)

    def block(args):
        i, qi, ci = args
        s = jnp.einsum('bqhd,bkhd->bhqk', qi, k, preferred_element_type=f32) * scale
        s = s + ci[..., None] - cT[:, :, None, :]
        qpos = i * Q_BLOCK + jnp.arange(Q_BLOCK)
        s = jnp.where((kpos[None, :] <= qpos[:, None])[None, None], s, -jnp.inf)
        p = jax.nn.softmax(s, axis=-1)
        return jnp.einsum('bhqk,bkhd->bqhd', p, v.astype(f32))

    o = lax.map(block, (jnp.arange(nb), qb, cb))
    return jnp.transpose(o, (1, 0, 2, 3, 4)).reshape(B, S, H * Dh)


def fox_sample(q, k, v, lf, cache_k, cache_v, cache_logf, layer, page_table):
    f32 = jnp.float32
    B, T, H, Dh = q.shape
    scale = FOX_HD ** -0.5
    P = page_table.shape[1] * PAGE_SIZE
    kp = cache_k[layer, page_table].reshape(B, P, H, Dh)
    vp = cache_v[layer, page_table].reshape(B, P, H, Dh)
    lfp = cache_logf[layer, page_table].reshape(B, P, H).astype(f32)
    suffix = lax.cumsum(lfp, axis=1, reverse=True) - lfp
    cn = jnp.cumsum(lf.astype(f32), axis=1)
    cnT = jnp.transpose(cn, (0, 2, 1))
    s_past = (jnp.einsum('bqhd,bkhd->bhqk', q, kp, preferred_element_type=f32) * scale
              + jnp.transpose(suffix, (0, 2, 1))[:, :, None, :] + cnT[..., None])
    s_new = (jnp.einsum('bqhd,bkhd->bhqk', q, k, preferred_element_type=f32) * scale
             + cnT[..., None] - cnT[:, :, None, :])
    tri = jnp.arange(T)[:, None] >= jnp.arange(T)[None, :]
    s_new = jnp.where(tri[None, None], s_new, -jnp.inf)
    p = jax.nn.softmax(jnp.concatenate([s_past, s_new], axis=-1), axis=-1)
    o = (jnp.einsum('bhqk,bkhd->bqhd', p[..., :P], vp.astype(f32))
         + jnp.einsum('bhqk,bkhd->bqhd', p[..., P:], v.astype(f32)))
    return o.reshape(B, T, H * Dh)


def layer_forward(x, wl, rw_S0, rw_last, pool_prefix, pos0, fox_fn):
    (g_pre, g_post, w_in, b_f, mu, w0, w2, a0, a2, k_k, k_a, r_k, ln_w, ln_b,
     pool_w, pool_scale, proj_a, proj_b, proj_c, w_out) = wl
    f32 = jnp.float32
    dt = x.dtype
    B, T, _ = x.shape
    xn = rms_norm(x, g_pre)
    z = jnp.einsum('btd,dn->btn', xn, w_in)
    z_rw, gate_rw, u_pool, gate_pool, qkv, f_logit, gate_fox, merge = jnp.split(z, _split_points(), axis=-1)
    o_rw, S_new = rwkv_mix(z_rw, rw_last, rw_S0, mu, w0, w2, a0, a2, k_k, k_a, r_k, ln_w, ln_b)
    o_a = (o_rw * jax.nn.silu(gate_rw.astype(f32))).astype(dt)
    o_pool, pool_buf = pool_mix(u_pool, pool_prefix, pos0, pool_w, pool_scale)
    o_b = (o_pool * jax.nn.silu(gate_pool.astype(f32))).astype(dt)
    q, k, v = [t.reshape(B, T, FOX_HEADS, FOX_HD) for t in jnp.split(qkv, 3, axis=-1)]
    lf = jax.nn.log_sigmoid((f_logit + b_f).astype(f32))
    o_fox = fox_fn(q, k, v, lf)
    o_c = (o_fox * jax.nn.silu(gate_fox.astype(f32))).astype(dt)
    g = jax.nn.sigmoid(merge.astype(f32)).reshape(B, T, N_BRANCH, D_MODEL).astype(dt)
    m = (g[:, :, 0] * jnp.einsum('btc,cd->btd', o_a, proj_a)
         + g[:, :, 1] * jnp.einsum('btc,cd->btd', o_b, proj_b)
         + g[:, :, 2] * jnp.einsum('btc,cd->btd', o_c, proj_c))
    y = jnp.einsum('btd,de->bte', m, w_out)
    x = x + rms_norm(y, g_post)
    return x, (S_new, z_rw[:, -1], pool_buf, k, v, lf)


def setup_inputs(seed: int = 0) -> dict:
    key = jax.random.key(seed)
    ks = jax.random.split(key, 32)
    f32 = jnp.float32
    n_pages = PAST_LEN // PAGE_SIZE
    n_used = DEC_BATCH * n_pages
    n_pool = n_used + max(1, n_used // 4)
    nrm = lambda k, shape, s: s * jax.random.normal(k, shape, f32)
    L = DEPTH
    fb = jnp.linspace(FORGET_BIAS_LO, FORGET_BIAS_HI, FOX_HEADS, dtype=f32)
    return {
        "x_prompt": nrm(ks[0], (BATCH, SEQ, D_MODEL), 1.0),
        "x_sample": nrm(ks[1], (DEC_BATCH, DEC_SEQ, D_MODEL), 1.0),
        "state_rwkv": nrm(ks[2], (L, DEC_BATCH, RW_HEADS, RW_HD, RW_HD), 0.5),
        "state_shift": nrm(ks[3], (L, DEC_BATCH, RW_SHIFT), 1.0),
        "state_pool": nrm(ks[4], (L, DEC_BATCH, POOL_BUF, POOL_W), 1.0),
        "cache_k": nrm(ks[5], (L, n_pool, PAGE_SIZE, FOX_HEADS, FOX_HD), 1.0),
        "cache_v": nrm(ks[6], (L, n_pool, PAGE_SIZE, FOX_HEADS, FOX_HD), 1.0),
        "cache_logf": jax.nn.log_sigmoid(fb + nrm(ks[7], (L, n_pool, PAGE_SIZE, FOX_HEADS), 1.0)),
        "page_table": jax.random.permutation(ks[8], n_pool)[:n_used].reshape(DEC_BATCH, n_pages).astype(jnp.int32),
        "norm_pre": 1.0 + nrm(ks[9], (L, D_MODEL), 0.05),
        "norm_post": 1.0 + nrm(ks[10], (L, D_MODEL), 0.05),
        "w_in": nrm(ks[11], (L, D_MODEL, N_IN), D_MODEL ** -0.5),
        "b_forget": fb + nrm(ks[12], (L, FOX_HEADS), 0.1),
        "rw_mu": jax.random.uniform(ks[13], (L, RW_SHIFT), f32),
        "rw_w0": jax.random.uniform(ks[14], (L, RW_W), f32, -4.0, 1.0),
        "rw_w2": nrm(ks[15], (L, LORA_W, RW_W), 0.05),
        "rw_a0": nrm(ks[16], (L, RW_W), 0.5),
        "rw_a2": nrm(ks[17], (L, LORA_A, RW_W), 0.05),
        "rw_kk": 0.85 + nrm(ks[18], (L, RW_W), 0.05),
        "rw_ka": 1.0 + nrm(ks[19], (L, RW_W), 0.05),
        "rw_rk": nrm(ks[20], (L, RW_W), 0.1),
        "rw_lnw": 1.0 + nrm(ks[21], (L, RW_W), 0.05),
        "rw_lnb": nrm(ks[22], (L, RW_W), 0.02),
        "pool_w": nrm(ks[23], (L, POOL_GROUPS, POOL_GW, POOL_GW), POOL_GW ** -0.5),
        "pool_scale": 1.0 + nrm(ks[24], (L, POOL_W), 0.05),
        "proj_a": nrm(ks[25], (L, RW_W, D_MODEL), RW_W ** -0.5),
        "proj_b": nrm(ks[26], (L, POOL_W, D_MODEL), POOL_W ** -0.5),
        "proj_c": nrm(ks[27], (L, FOX_W, D_MODEL), FOX_W ** -0.5),
        "w_out": nrm(ks[28], (L, D_MODEL, D_MODEL), D_MODEL ** -0.5),
    }


def reference(x_prompt, x_sample, state_rwkv, state_shift, state_pool, cache_k, cache_v, cache_logf,
              page_table, norm_pre, norm_post, w_in, b_forget, rw_mu, rw_w0, rw_w2, rw_a0, rw_a2,
              rw_kk, rw_ka, rw_rk, rw_lnw, rw_lnb, pool_w, pool_scale, proj_a, proj_b, proj_c, w_out):
    hp, hs = x_prompt, x_sample
    Bp = x_prompt.shape[0]
    past = page_table.shape[1] * PAGE_SIZE
    rw_p, rw_s, sh_p, sh_s, pl_p, pl_s = [], [], [], [], [], []
    k_p, k_s, v_p, v_s, lf_p, lf_s = [], [], [], [], [], []
    for l in range(DEPTH):
        wl = (norm_pre[l], norm_post[l], w_in[l], b_forget[l], rw_mu[l], rw_w0[l], rw_w2[l], rw_a0[l],
              rw_a2[l], rw_kk[l], rw_ka[l], rw_rk[l], rw_lnw[l], rw_lnb[l], pool_w[l], pool_scale[l],
              proj_a[l], proj_b[l], proj_c[l], w_out[l])
        hp, st_p = layer_forward(
            hp, wl,
            jnp.zeros((Bp, RW_HEADS, RW_HD, RW_HD), jnp.float32),
            jnp.zeros((Bp, RW_SHIFT), hp.dtype),
            jnp.zeros((Bp, POOL_BUF, POOL_W), hp.dtype),
            0, fox_prompt)
        fox_s = lambda q, k, v, lf, l=l: fox_sample(q, k, v, lf, cache_k, cache_v, cache_logf, l, page_table)
        hs, st_s = layer_forward(hs, wl, state_rwkv[l], state_shift[l], state_pool[l], past, fox_s)
        rw_p.append(st_p[0]); sh_p.append(st_p[1]); pl_p.append(st_p[2])
        k_p.append(st_p[3]); v_p.append(st_p[4]); lf_p.append(st_p[5])
        rw_s.append(st_s[0]); sh_s.append(st_s[1]); pl_s.append(st_s[2])
        k_s.append(st_s[3]); v_s.append(st_s[4]); lf_s.append(st_s[5])
    return (hp, hs,
            jnp.stack(rw_p), jnp.stack(rw_s),
            jnp.stack(sh_p), jnp.stack(sh_s),
            jnp.stack(pl_p), jnp.stack(pl_s),
            jnp.stack(k_p), jnp.stack(k_s),
            jnp.stack(v_p), jnp.stack(v_s),
            jnp.stack(lf_p), jnp.stack(lf_s))
```

```python
import functools

import jax
import jax.numpy as jnp
from jax import lax
from jax.experimental import pallas as pl
from jax.experimental.pallas import tpu as pltpu

F32 = jnp.float32
BF16 = jnp.bfloat16
HIGHEST = lax.Precision.HIGHEST

D_MODEL = 1024
HEADS = 8
HEAD_DIM = 64
BRANCH_W = HEADS * HEAD_DIM
LORA = 64
RW_SHIFT = 3 * BRANCH_W + 2 * LORA
POOL_WINDOWS = (2, 4, 8, 16)
POOL_GW = 128
POOL_BUF = 15
PAGE = 128
CHUNK = 64
LANES = 128
EPS = 1e-6
LNX_EPS = 64e-5

C_MERGE = 0
C_QKV = 3 * D_MODEL
C_GRW = C_QKV + 3 * BRANCH_W
C_GPOOL = C_GRW + BRANCH_W
C_GFOX = C_GPOOL + BRANCH_W
C_UPOOL = C_GFOX + BRANCH_W
C_ZRW = C_UPOOL + BRANCH_W
C_F = C_ZRW + RW_SHIFT
W_COLS = C_F + LANES

VMEM_LIMIT = 56 * 1024 * 1024


def _cparams(sem):
    return pltpu.CompilerParams(dimension_semantics=sem, vmem_limit_bytes=VMEM_LIMIT)


def _sigmoid(x):
    return 1.0 / (1.0 + jnp.exp(-x))


def _log_sigmoid(x):
    return jnp.minimum(x, 0.0) - jnp.log1p(jnp.exp(-jnp.abs(x)))


def _dot(a, b):
    return jnp.dot(a.astype(BF16), b.astype(BF16), preferred_element_type=F32)


def _dot_nt(a, b):
    return lax.dot_general(a.astype(BF16), b.astype(BF16), (((1,), (1,)), ((), ())),
                           preferred_element_type=F32)


def _dot_tn(a, b):
    return lax.dot_general(a.astype(BF16), b.astype(BF16), (((0,), (0,)), ((), ())),
                           preferred_element_type=F32)


def _inproj_kernel(x_ref, g_ref, w_ref, bf_ref, gm_ref, q_ref, k_ref, v_ref, kb_ref, vb_ref,
                   grw_ref, gpool_ref, gfox_ref, u_ref, zrw_ref, lf_ref):
    x = x_ref[...]
    ms = jnp.mean(x * x, axis=-1, keepdims=True)
    xn = (x * lax.rsqrt(ms + EPS) * g_ref[...]).astype(BF16)

    def mm(c0, width):
        return jnp.dot(xn, w_ref[:, c0:c0 + width], preferred_element_type=F32)

    for i in range(3):
        gm_ref[:, i * D_MODEL:(i + 1) * D_MODEL] = _sigmoid(mm(C_MERGE + i * D_MODEL, D_MODEL))
    q_ref[...] = (mm(C_QKV, BRANCH_W) * (HEAD_DIM ** -0.5)).astype(BF16)
    k = mm(C_QKV + BRANCH_W, BRANCH_W)
    k_ref[...] = k
    kb_ref[...] = k.astype(BF16)
    v = mm(C_QKV + 2 * BRANCH_W, BRANCH_W)
    v_ref[...] = v
    vb_ref[...] = v.astype(BF16)
    for ref, c0 in ((grw_ref, C_GRW), (gpool_ref, C_GPOOL), (gfox_ref, C_GFOX)):
        g = mm(c0, BRANCH_W)
        ref[...] = g * _sigmoid(g)
    u_ref[...] = mm(C_UPOOL, BRANCH_W)
    zrw_ref[:, 0:3 * BRANCH_W] = mm(C_ZRW, 3 * BRANCH_W)
    zrw_ref[:, 3 * BRANCH_W:RW_SHIFT] = mm(C_ZRW + 3 * BRANCH_W, 2 * LORA)
    f = mm(C_F, LANES) + bf_ref[...]
    head_lane = lax.broadcasted_iota(jnp.int32, f.shape, 1) < HEADS
    lf_ref[...] = jnp.where(head_lane, _log_sigmoid(f), 0.0)


def _inproj(x2d, g_pre, w_cat, bf_pad, tm):
    n = x2d.shape[0]
    row = lambda i: (i, 0)
    const = lambda i: (0, 0)
    widths = [(3 * D_MODEL, F32), (BRANCH_W, BF16), (BRANCH_W, F32), (BRANCH_W, F32),
              (BRANCH_W, BF16), (BRANCH_W, BF16), (BRANCH_W, F32), (BRANCH_W, F32),
              (BRANCH_W, F32), (BRANCH_W, F32), (RW_SHIFT, F32), (LANES, F32)]
    return pl.pallas_call(
        _inproj_kernel,
        grid=(n // tm,),
        in_specs=[pl.BlockSpec((tm, D_MODEL), row),
                  pl.BlockSpec((1, D_MODEL), const),
                  pl.BlockSpec((D_MODEL, W_COLS), const, pipeline_mode=pl.Buffered(1)),
                  pl.BlockSpec((1, LANES), const)],
        out_specs=[pl.BlockSpec((tm, w), row) for w, _ in widths],
        out_shape=[jax.ShapeDtypeStruct((n, w), dt) for w, dt in widths],
        compiler_params=_cparams(("parallel",)),
        name="inproj",
    )(x2d, g_pre, w_cat, bf_pad)


def _seg_sum(x, lo_mask):
    s_lo = jnp.sum(jnp.where(lo_mask, x, 0.0), axis=-1, keepdims=True)
    s_hi = jnp.sum(jnp.where(lo_mask, 0.0, x), axis=-1, keepdims=True)
    return jnp.where(lo_mask, s_lo, s_hi)


def _seg_sum_wide(x):
    tt = x.shape[0]
    lo_mask = lax.broadcasted_iota(jnp.int32, (tt, LANES), 1) < HEAD_DIM
    parts = [_seg_sum(x[:, p * LANES:(p + 1) * LANES], lo_mask) for p in range(x.shape[1] // LANES)]
    return jnp.concatenate(parts, axis=-1)


def _rwkv_pre_kernel(z_ref, zp_ref, zl_ref, mu_ref, w0_ref, w2_ref, a0_ref, a2_ref, kk_ref, ka_ref,
                     rk_ref, at_ref, rt_ref, bt_ref, kt_ref, vb_ref, bonus_ref, pc_ref,
                     *, tt, t_seq, t_valid):
    i = pl.program_id(0)
    z = z_ref[...]
    starts_seq = (i * tt) % t_seq == 0
    prev_row = jnp.where(starts_seq, zl_ref[...], zp_ref[7:8, :])
    row = lax.broadcasted_iota(jnp.int32, (tt, 1), 0)
    z_prev = jnp.where(row == 0, prev_row, pltpu.roll(z, 1, axis=0))
    zs = z + (z_prev - z) * mu_ref[...]
    w3 = BRANCH_W
    r = zs[:, 0:w3]
    k = zs[:, w3:2 * w3]
    v = zs[:, 2 * w3:3 * w3]
    lora = zs[:, 3 * w3:RW_SHIFT]
    wx = w0_ref[...] + jnp.dot(jnp.tanh(lora).astype(BF16), w2_ref[...], preferred_element_type=F32)
    logd = -jnp.exp(_log_sigmoid(wx) - 0.5)
    a = _sigmoid(a0_ref[...] + jnp.dot(lora.astype(BF16), a2_ref[...], preferred_element_type=F32))
    kk = k * kk_ref[...]
    k2 = k * (1.0 + (a - 1.0) * ka_ref[...])
    kk = kk / jnp.maximum(jnp.sqrt(_seg_sum_wide(kk * kk)), 1e-12)
    bonus_ref[...] = _seg_sum_wide(r * k2 * rk_ref[...]) * v
    if t_valid < CHUNK:
        valid = (row % CHUNK) < t_valid
        logd = jnp.where(valid, logd, 0.0)
        kk = jnp.where(valid, kk, 0.0)
        k2 = jnp.where(valid, k2, 0.0)
        r = jnp.where(valid, r, 0.0)
    ri = lax.broadcasted_iota(jnp.int32, (CHUNK, CHUNK), 0)
    ci = lax.broadcasted_iota(jnp.int32, (CHUNK, CHUNK), 1)
    tri = jnp.where(ci <= ri, 1.0, 0.0).astype(F32)
    cl = jnp.concatenate(
        [jnp.dot(tri, logd[c * CHUNK:(c + 1) * CHUNK, :], preferred_element_type=F32, precision=HIGHEST)
         for c in range(tt // CHUNK)], axis=0)
    e_neg = jnp.exp(-cl)
    at_ref[...] = (-kk * jnp.exp(cl - logd)).astype(BF16)
    rt_ref[...] = (r * jnp.exp(cl)).astype(BF16)
    bt_ref[...] = (kk * a * e_neg).astype(BF16)
    kt_ref[...] = (k2 * e_neg).astype(BF16)
    vb_ref[...] = v.astype(BF16)
    for c in range(tt // CHUNK):
        last = cl[(c + 1) * CHUNK - 1:(c + 1) * CHUNK, :]
        pc_ref[c * 8:(c + 1) * 8, :] = jnp.broadcast_to(jnp.exp(last), (8, BRANCH_W))


def _rwkv_pre(z, z_last, wl, tt, t_seq, t_valid):
    n = z.shape[0]
    row = lambda i: (i, 0)
    const = lambda i: (0, 0)
    vec = lambda w: pl.BlockSpec((1, w), const)
    nb8 = tt // 8
    kern = functools.partial(_rwkv_pre_kernel, tt=tt, t_seq=t_seq, t_valid=t_valid)
    wide = lambda dt: jax.ShapeDtypeStruct((n, BRANCH_W), dt)
    return pl.pallas_call(
        kern,
        grid=(n // tt,),
        in_specs=[pl.BlockSpec((tt, RW_SHIFT), row),
                  pl.BlockSpec((8, RW_SHIFT), lambda i: (jnp.maximum(i * nb8 - 1, 0), 0)),
                  pl.BlockSpec((None, 1, RW_SHIFT), lambda i: ((i * tt) // t_seq, 0, 0)),
                  vec(RW_SHIFT), vec(BRANCH_W),
                  pl.BlockSpec((LANES, BRANCH_W), const),
                  vec(BRANCH_W),
                  pl.BlockSpec((LANES, BRANCH_W), const),
                  vec(BRANCH_W), vec(BRANCH_W), vec(BRANCH_W)],
        out_specs=[pl.BlockSpec((tt, BRANCH_W), row)] * 6
                  + [pl.BlockSpec((tt // CHUNK * 8, BRANCH_W), row)],
        out_shape=[wide(BF16), wide(BF16), wide(BF16), wide(BF16), wide(BF16), wide(F32),
                   jax.ShapeDtypeStruct((n // CHUNK * 8, BRANCH_W), F32)],
        compiler_params=_cparams(("parallel",)),
        name="rwkv_pre",
    )(z, z, z_last, wl["mu"], wl["w0"], wl["w2p"], wl["a0"], wl["a2p"], wl["kk"], wl["ka"], wl["rk"])


def _rwkv_chunk_kernel(at_ref, rt_ref, bt_ref, kt_ref, vb_ref, bonus_ref, pc_ref, s0_ref, lnw_ref,
                       lnb_ref, o_ref, sout_ref, s_scr, *, cpb):
    g = pl.program_id(2)
    c2 = 2 * CHUNK

    @pl.when(g == 0)
    def _():
        s_scr[...] = jnp.concatenate([s0_ref[0], s0_ref[1]], axis=1)

    lane = lax.broadcasted_iota(jnp.int32, (CHUNK, LANES), 1)
    lo = lane < HEAD_DIM
    ri = lax.broadcasted_iota(jnp.int32, (c2, c2), 0)
    ci = lax.broadcasted_iota(jnp.int32, (c2, c2), 1)
    same = (ri // CHUNK) == (ci // CHUNK)
    strict = same & (ci < ri)
    incl = same & (ci <= ri)
    eye = jnp.where(ri == ci, 1.0, 0.0).astype(F32)

    def stack2(x):
        x = x.astype(F32)
        return jnp.concatenate([jnp.where(lo, x, 0.0), jnp.where(lo, 0.0, x)], axis=0)

    def body(c, carry):
        r0 = pl.multiple_of(c * CHUNK, CHUNK)
        rows = pl.ds(r0, CHUNK)
        a_s = stack2(at_ref[rows, :])
        r_s = stack2(rt_ref[rows, :])
        b_s = stack2(bt_ref[rows, :])
        k_s = stack2(kt_ref[rows, :])
        vp = vb_ref[rows, :].astype(F32)
        v_s = jnp.concatenate([vp[:, :HEAD_DIM], vp[:, HEAD_DIM:]], axis=0)
        pc = pc_ref[pl.ds(pl.multiple_of(c * 8, 8), 1), :]
        nmat = jnp.where(strict, _dot_nt(a_s, b_s), 0.0)
        lak = jnp.where(strict, _dot_nt(a_s, k_s), 0.0)
        mrb = jnp.where(incl, _dot_nt(r_s, b_s), 0.0)
        mrk = jnp.where(incl, _dot_nt(r_s, k_s), 0.0)
        tinv = eye + nmat
        pw = nmat
        for _ in range(5):
            pw = _dot(pw, pw)
            tinv = tinv + _dot(tinv, pw)
        a_hat = _dot(tinv, a_s)
        u_hat = _dot(tinv, _dot(lak, v_s))
        r_hat = r_s + _dot(mrb, a_hat)
        y_hat = _dot(mrb, u_hat) + _dot(mrk, v_s)
        gmat = (eye + _dot_tn(a_hat, b_s)) * pc
        hmat = (_dot_tn(u_hat, b_s) + _dot_tn(v_s, k_s)) * pc
        s = s_scr[...]
        y_s = _dot_nt(r_hat, s) + y_hat
        s_scr[...] = _dot(s, gmat) + hmat
        y = jnp.concatenate([y_s[:CHUNK], y_s[CHUNK:]], axis=1)
        mean = _seg_sum(y, lo) * (1.0 / HEAD_DIM)
        yc = y - mean
        var = _seg_sum(yc * yc, lo) * (1.0 / HEAD_DIM)
        o_ref[rows, :] = yc * lax.rsqrt(var + LNX_EPS) * lnw_ref[...] + lnb_ref[...] + bonus_ref[rows, :]
        return carry

    lax.fori_loop(0, cpb, body, 0)

    @pl.when(g == pl.num_programs(2) - 1)
    def _():
        s = s_scr[...]
        sout_ref[0] = s[:, :HEAD_DIM]
        sout_ref[1] = s[:, HEAD_DIM:]


def _rwkv_chunk(pre, s0, lnw, lnb, n_seq, t_seq, cpb):
    at, rt, bt, kt, vb, bonus, pc = pre
    n = at.shape[0]
    ng = t_seq // (cpb * CHUNK)
    pairs = HEADS // 2
    rows = cpb * CHUNK
    tile = lambda b, p, g: (b * ng + g, p)
    kern = functools.partial(_rwkv_chunk_kernel, cpb=cpb)
    return pl.pallas_call(
        kern,
        grid=(n_seq, pairs, ng),
        in_specs=[pl.BlockSpec((rows, LANES), tile)] * 6
                 + [pl.BlockSpec((cpb * 8, LANES), tile),
                    pl.BlockSpec((None, 2, HEAD_DIM, HEAD_DIM), lambda b, p, g: (b, p, 0, 0)),
                    pl.BlockSpec((1, LANES), lambda b, p, g: (0, p)),
                    pl.BlockSpec((1, LANES), lambda b, p, g: (0, p))],
        out_specs=[pl.BlockSpec((rows, LANES), tile),
                   pl.BlockSpec((None, 2, HEAD_DIM, HEAD_DIM), lambda b, p, g: (b, p, 0, 0))],
        out_shape=[jax.ShapeDtypeStruct((n, BRANCH_W), F32),
                   jax.ShapeDtypeStruct((n_seq, HEADS, HEAD_DIM, HEAD_DIM), F32)],
        scratch_shapes=[pltpu.VMEM((HEAD_DIM, LANES), F32)],
        compiler_params=_cparams(("parallel", "parallel", "arbitrary")),
        name="rwkv_chunk",
    )(at, rt, bt, kt, vb, bonus, pc, s0, lnw, lnb)


def _pool_prompt_kernel(u_ref, pw_ref, ps_ref, o_ref, buf_ref, ext_ref, *, t_seq, tile):
    pad = POOL_BUF + 1
    ext_ref[0:pad, :] = jnp.zeros((pad, BRANCH_W), F32)
    ext_ref[pad:pad + t_seq, :] = u_ref[...]
    buf_ref[...] = u_ref[t_seq - POOL_BUF:t_seq, :]
    for t0 in range(0, t_seq, tile):
        pos = t0 + lax.broadcasted_iota(jnp.int32, (tile, 1), 0)
        for gi, w in enumerate(POOL_WINDOWS):
            sl = slice(gi * POOL_GW, (gi + 1) * POOL_GW)
            wsum = ext_ref[pad + t0:pad + t0 + tile, sl]
            for j in range(1, w):
                wsum = wsum + ext_ref[pad + t0 - j:pad + t0 - j + tile, sl]
            cnt = jnp.minimum(pos + 1, w).astype(F32)
            d = wsum / cnt - ext_ref[pad + t0:pad + t0 + tile, sl]
            o_ref[t0:t0 + tile, sl] = _dot(d, pw_ref[gi]) * ps_ref[:, sl]


def _pool_prompt(u, pool_w, pool_scale, n_seq, t_seq):
    kern = functools.partial(_pool_prompt_kernel, t_seq=t_seq, tile=256)
    return pl.pallas_call(
        kern,
        grid=(n_seq,),
        in_specs=[pl.BlockSpec((t_seq, BRANCH_W), lambda b: (b, 0)),
                  pl.BlockSpec((len(POOL_WINDOWS), POOL_GW, POOL_GW), lambda b: (0, 0, 0)),
                  pl.BlockSpec((1, BRANCH_W), lambda b: (0, 0))],
        out_specs=[pl.BlockSpec((t_seq, BRANCH_W), lambda b: (b, 0)),
                   pl.BlockSpec((None, POOL_BUF, BRANCH_W), lambda b: (b, 0, 0))],
        out_shape=[jax.ShapeDtypeStruct((n_seq * t_seq, BRANCH_W), F32),
                   jax.ShapeDtypeStruct((n_seq, POOL_BUF, BRANCH_W), F32)],
        scratch_shapes=[pltpu.VMEM((POOL_BUF + 1 + t_seq, BRANCH_W), F32)],
        compiler_params=_cparams(("parallel",)),
        name="pool_prompt",
    )(u, pool_w, pool_scale)


def _pool_sample_kernel(pre_ref, u_ref, pw_ref, ps_ref, o_ref, buf_ref, *, t_new, pos0):
    def ext_row(i, sl):
        if i < POOL_BUF:
            return pre_ref[i, :, sl]
        return u_ref[i - POOL_BUF, :, sl]

    full = slice(0, BRANCH_W)
    for i in range(POOL_BUF):
        buf_ref[i] = ext_row(i + t_new, full)
    for t in range(t_new):
        for gi, w in enumerate(POOL_WINDOWS):
            sl = slice(gi * POOL_GW, (gi + 1) * POOL_GW)
            cur = ext_row(POOL_BUF + t, sl)
            wsum = cur
            for j in range(1, w):
                wsum = wsum + ext_row(POOL_BUF + t - j, sl)
            cnt = float(min(pos0 + t + 1, w))
            d = wsum / cnt - cur
            o_ref[t, :, sl] = _dot(d, pw_ref[gi]) * ps_ref[:, sl]


def _pool_sample(prefix, u3, pool_w, pool_scale, pos0):
    t_new = u3.shape[0]
    kern = functools.partial(_pool_sample_kernel, t_new=t_new, pos0=pos0)
    full3 = lambda shape: pl.BlockSpec(shape, lambda i: (0, 0, 0))
    return pl.pallas_call(
        kern,
        grid=(1,),
        in_specs=[full3(prefix.shape), full3(u3.shape), full3(pool_w.shape),
                  pl.BlockSpec((1, BRANCH_W), lambda i: (0, 0))],
        out_specs=[full3(u3.shape), full3(prefix.shape)],
        out_shape=[jax.ShapeDtypeStruct(u3.shape, F32), jax.ShapeDtypeStruct(prefix.shape, F32)],
        compiler_params=_cparams(("arbitrary",)),
        name="pool_sample",
    )(prefix, u3, pool_w, pool_scale)


def _fox_cum_kernel(lf_ref, c_ref, ct_ref, *, t_seq, tile):
    ri = lax.broadcasted_iota(jnp.int32, (tile, tile), 0)
    ci = lax.broadcasted_iota(jnp.int32, (tile, tile), 1)
    tri = jnp.where(ci <= ri, 1.0, 0.0).astype(F32)
    e8 = jnp.where(lax.broadcasted_iota(jnp.int32, (HEADS, LANES), 0)
                   == lax.broadcasted_iota(jnp.int32, (HEADS, LANES), 1), 1.0, 0.0).astype(F32)
    carry = jnp.zeros((1, LANES), F32)
    for t0 in range(0, t_seq, tile):
        c = jnp.dot(tri, lf_ref[t0:t0 + tile, :], preferred_element_type=F32, precision=HIGHEST) + carry
        c_ref[t0:t0 + tile, :] = c
        ct_ref[:, t0:t0 + tile] = lax.dot_general(e8, c, (((1,), (1,)), ((), ())),
                                                  preferred_element_type=F32, precision=HIGHEST)
        carry = c[tile - 1:tile, :]


def _fox_cum(lf, n_seq, t_seq):
    kern = functools.partial(_fox_cum_kernel, t_seq=t_seq, tile=256)
    return pl.pallas_call(
        kern,
        grid=(n_seq,),
        in_specs=[pl.BlockSpec((t_seq, LANES), lambda b: (b, 0))],
        out_specs=[pl.BlockSpec((t_seq, LANES), lambda b: (b, 0)),
                   pl.BlockSpec((None, HEADS, t_seq), lambda b: (b, 0, 0))],
        out_shape=[jax.ShapeDtypeStruct((n_seq * t_seq, LANES), F32),
                   jax.ShapeDtypeStruct((n_seq, HEADS, t_seq), F32)],
        compiler_params=_cparams(("parallel",)),
        name="fox_cum",
    )(lf)


def _fox_prompt_kernel(q_ref, k_ref, v_ref, c_ref, ct_ref, o_ref, *, tq):
    p = pl.program_id(1)
    qi = pl.program_id(2)
    q = q_ref[...].astype(F32)
    lane = lax.broadcasted_iota(jnp.int32, (tq, LANES), 1)
    lo = lane < HEAD_DIM
    ri = lax.broadcasted_iota(jnp.int32, (tq, tq), 0)
    ci = lax.broadcasted_iota(jnp.int32, (tq, tq), 1)
    causal = ci <= ri
    outs = []
    for hh in range(2):
        qh = (jnp.where(lo, q, 0.0) if hh == 0 else jnp.where(lo, 0.0, q)).astype(BF16)
        c_all = c_ref[...]
        cq = jnp.sum(jnp.where(lane == 2 * p + hh, c_all, 0.0), axis=-1, keepdims=True)

        def scores(j):
            cols = pl.ds(pl.multiple_of(j * tq, tq), tq)
            s = lax.dot_general(qh, k_ref[cols, :], (((1,), (1,)), ((), ())),
                                preferred_element_type=F32)
            ck = ct_ref[pl.ds(2 * p + hh, 1), cols]
            return s + cq - ck, cols

        def update(carry, s, cols):
            m, l, acc = carry
            m_new = jnp.maximum(m, jnp.max(s, axis=-1, keepdims=True))
            alpha = jnp.exp(m - m_new)
            pr = jnp.exp(s - m_new)
            l = alpha * l + jnp.sum(pr, axis=-1, keepdims=True)
            acc = alpha * acc + jnp.dot(pr.astype(BF16), v_ref[cols, :], preferred_element_type=F32)
            return m_new, l, acc

        def body(j, carry):
            s, cols = scores(j)
            return update(carry, s, cols)

        init = (jnp.full((tq, 1), -jnp.inf, F32), jnp.zeros((tq, 1), F32), jnp.zeros((tq, LANES), F32))
        carry = lax.fori_loop(0, qi, body, init)
        s, cols = scores(qi)
        s = jnp.where(causal, s, -jnp.inf)
        m, l, acc = update(carry, s, cols)
        outs.append(acc / l)
    o_ref[...] = jnp.where(lo, outs[0], outs[1])


def _fox_prompt(qb, kb, vb, c, ct, n_seq, t_seq, tq):
    nq = t_seq // tq
    pairs = HEADS // 2
    kern = functools.partial(_fox_prompt_kernel, tq=tq)
    return pl.pallas_call(
        kern,
        grid=(n_seq, pairs, nq),
        in_specs=[pl.BlockSpec((tq, LANES), lambda b, p, i: (b * nq + i, p)),
                  pl.BlockSpec((t_seq, LANES), lambda b, p, i: (b, p)),
                  pl.BlockSpec((t_seq, LANES), lambda b, p, i: (b, p)),
                  pl.BlockSpec((tq, LANES), lambda b, p, i: (b * nq + i, 0)),
                  pl.BlockSpec((None, HEADS, t_seq), lambda b, p, i: (b, 0, 0))],
        out_specs=pl.BlockSpec((tq, LANES), lambda b, p, i: (b * nq + i, p)),
        out_shape=jax.ShapeDtypeStruct((n_seq * t_seq, BRANCH_W), F32),
        compiler_params=_cparams(("parallel", "parallel", "arbitrary")),
        name="fox_prompt",
    )(qb, kb, vb, c, ct)


def _fox_sample_kernel(pt_ref, q_ref, kn_ref, vn_ref, lfn_ref, *refs, n_pages, t_new):
    del pt_ref
    k_pages = refs[0:n_pages]
    v_pages = refs[n_pages:2 * n_pages]
    lf_pages = refs[2 * n_pages:3 * n_pages]
    o_ref = refs[3 * n_pages]
    past = n_pages * PAGE
    rows = t_new * HEADS
    nt = (((1,), (1,)), ((), ()))
    lf_t = jnp.concatenate([lf_pages[i][...] for i in range(n_pages)], axis=1)
    pos = lax.broadcasted_iota(jnp.int32, (HEADS, past), 1)
    x = lf_t
    sh = 1
    while sh < past:
        x = x + jnp.where(pos + sh < past, pltpu.roll(x, past - sh, axis=1), 0.0)
        sh *= 2
    suffix = x - lf_t
    lfn = lfn_ref[...]
    cn = [lfn[:, 0:1]]
    for t in range(1, t_new):
        cn.append(cn[-1] + lfn[:, t:t + 1])
    q = q_ref[...].astype(F32)
    head_of_lane = lax.broadcasted_iota(jnp.int32, (HEADS, BRANCH_W), 1) // HEAD_DIM
    head_mask = head_of_lane == lax.broadcasted_iota(jnp.int32, (HEADS, BRANCH_W), 0)
    qbd = jnp.concatenate(
        [jnp.where(head_mask, jnp.broadcast_to(q[t:t + 1, :], (HEADS, BRANCH_W)), 0.0)
         for t in range(t_new)], axis=0)
    qbd_bf = qbd.astype(BF16)
    s_past = jnp.concatenate(
        [lax.dot_general(qbd_bf, k_pages[i][...].astype(BF16), nt, preferred_element_type=F32)
         for i in range(n_pages)], axis=1)
    s_past = s_past + jnp.concatenate([suffix + cn[t] for t in range(t_new)], axis=0)
    row_t = lax.broadcasted_iota(jnp.int32, (rows, 1), 0) // HEADS
    kn = kn_ref[...]
    vn = vn_ref[...]
    s_new = []
    for j in range(t_new):
        dots = jnp.sum(qbd * kn[j:j + 1, :], axis=-1, keepdims=True)
        bias = jnp.concatenate([cn[t] - cn[j] for t in range(t_new)], axis=0)
        s_new.append(jnp.where(row_t >= j, dots + bias, -jnp.inf))
    m = jnp.max(s_past, axis=-1, keepdims=True)
    for j in range(t_new):
        m = jnp.maximum(m, s_new[j])
    p_past = jnp.exp(s_past - m)
    l = jnp.sum(p_past, axis=-1, keepdims=True)
    o = jnp.zeros((rows, BRANCH_W), F32)
    for j in range(t_new):
        p_j = jnp.exp(s_new[j] - m)
        l = l + p_j
        o = o + p_j * vn[j:j + 1, :]
    for i in range(n_pages):
        o = o + jnp.dot(p_past[:, i * PAGE:(i + 1) * PAGE].astype(BF16), v_pages[i][...].astype(BF16),
                        preferred_element_type=F32)
    o = o / l
    for t in range(t_new):
        blk = jnp.where(head_mask, o[t * HEADS:(t + 1) * HEADS, :], 0.0)
        o_ref[t:t + 1, :] = jnp.sum(blk, axis=0, keepdims=True)


def _fox_sample(layer, page_table, q3, k3, v3, lf3t, cache_k, cache_v, cache_logf_t):
    n_seq, t_new, _ = q3.shape
    n_pages = page_table.shape[1]
    kern = functools.partial(_fox_sample_kernel, n_pages=n_pages, t_new=t_new)
    new_spec = lambda w: pl.BlockSpec((None, t_new, w), lambda b, pt: (b, 0, 0))

    def page_spec(i, rows, w):
        return pl.BlockSpec((None, None, rows, w), lambda b, pt: (layer, pt[b, i], 0, 0))

    grid_spec = pltpu.PrefetchScalarGridSpec(
        num_scalar_prefetch=1,
        grid=(n_seq,),
        in_specs=[new_spec(BRANCH_W), new_spec(BRANCH_W), new_spec(BRANCH_W),
                  pl.BlockSpec((None, HEADS, t_new), lambda b, pt: (b, 0, 0))]
                 + [page_spec(i, PAGE, BRANCH_W) for i in range(n_pages)]
                 + [page_spec(i, PAGE, BRANCH_W) for i in range(n_pages)]
                 + [page_spec(i, HEADS, PAGE) for i in range(n_pages)],
        out_specs=pl.BlockSpec((None, t_new, BRANCH_W), lambda b, pt: (b, 0, 0)),
    )
    return pl.pallas_call(
        kern,
        grid_spec=grid_spec,
        out_shape=jax.ShapeDtypeStruct((n_seq, t_new, BRANCH_W), F32),
        compiler_params=_cparams(("parallel",)),
        name="fox_sample",
    )(page_table, q3, k3, v3, lf3t, *([cache_k] * n_pages), *([cache_v] * n_pages),
      *([cache_logf_t] * n_pages))


def _outproj_kernel(x_ref, oa_ref, ob_ref, oc_ref, ga_ref, gb_ref, gc_ref, gm_ref, pa_ref, pb_ref,
                    pc_ref, wo_ref, gp_ref, y_ref):
    ya = _dot(oa_ref[...] * ga_ref[...], pa_ref[...])
    yb = _dot(ob_ref[...] * gb_ref[...], pb_ref[...])
    yc = _dot(oc_ref[...] * gc_ref[...], pc_ref[...])
    m = (gm_ref[:, 0:D_MODEL] * ya + gm_ref[:, D_MODEL:2 * D_MODEL] * yb
         + gm_ref[:, 2 * D_MODEL:3 * D_MODEL] * yc)
    y = _dot(m, wo_ref[...])
    ms = jnp.mean(y * y, axis=-1, keepdims=True)
    y_ref[...] = x_ref[...] + y * lax.rsqrt(ms + EPS) * gp_ref[...]


def _outproj(x2d, oa, ob, oc, ga, gb, gc, gm, wl, tm):
    n = x2d.shape[0]
    row = lambda i: (i, 0)
    const = lambda i: (0, 0)
    wide = pl.BlockSpec((tm, BRANCH_W), row)
    proj = pl.BlockSpec((BRANCH_W, D_MODEL), const)
    return pl.pallas_call(
        _outproj_kernel,
        grid=(n // tm,),
        in_specs=[pl.BlockSpec((tm, D_MODEL), row), wide, wide, wide, wide, wide, wide,
                  pl.BlockSpec((tm, 3 * D_MODEL), row), proj, proj, proj,
                  pl.BlockSpec((D_MODEL, D_MODEL), const), pl.BlockSpec((1, D_MODEL), const)],
        out_specs=pl.BlockSpec((tm, D_MODEL), row),
        out_shape=jax.ShapeDtypeStruct((n, D_MODEL), F32),
        compiler_params=_cparams(("parallel",)),
        name="outproj",
    )(x2d, oa, ob, oc, ga, gb, gc, gm, wl["proj_a"], wl["proj_b"], wl["proj_c"], wl["w_out"],
      wl["g_post"])


def _layer_weights(l, w_cat, p):
    row = lambda a: a[l][None, :]
    zeros = jnp.zeros((LORA, BRANCH_W), BF16)
    return {
        "w_cat": w_cat[l],
        "g_pre": row(p["norm_pre"]), "g_post": row(p["norm_post"]),
        "bf_pad": jnp.pad(p["b_forget"][l], (0, LANES - HEADS))[None, :],
        "mu": row(p["rw_mu"]), "w0": row(p["rw_w0"]), "a0": row(p["rw_a0"]),
        "w2p": jnp.concatenate([p["rw_w2"][l].astype(BF16), zeros], axis=0),
        "a2p": jnp.concatenate([zeros, p["rw_a2"][l].astype(BF16)], axis=0),
        "kk": row(p["rw_kk"]), "ka": row(p["rw_ka"]), "rk": row(p["rw_rk"]),
        "lnw": row(p["rw_lnw"]), "lnb": row(p["rw_lnb"]),
        "pool_w": p["pool_w"][l].astype(BF16), "pool_scale": row(p["pool_scale"]),
        "proj_a": p["proj_a"][l].astype(BF16), "proj_b": p["proj_b"][l].astype(BF16),
        "proj_c": p["proj_c"][l].astype(BF16), "w_out": p["w_out"][l].astype(BF16),
    }


def _reorder_w_in(w_in):
    a = RW_SHIFT
    segs = {"z_rw": (0, a), "g_rw": (a, a + 512), "u": (a + 512, a + 1024), "g_pool": (a + 1024, a + 1536),
            "qkv": (a + 1536, a + 3072), "f": (a + 3072, a + 3080), "g_fox": (a + 3080, a + 3592),
            "merge": (a + 3592, a + 3592 + 3 * D_MODEL)}
    order = ["merge", "qkv", "g_rw", "g_pool", "g_fox", "u", "z_rw", "f"]
    cols = [w_in[..., segs[s][0]:segs[s][1]] for s in order]
    cols.append(jnp.zeros(w_in.shape[:-1] + (LANES - HEADS,), w_in.dtype))
    return jnp.concatenate(cols, axis=-1).astype(BF16)


def _prompt_layer(x2d, wl, n_seq, t_seq):
    (gm, qb, k, v, kb, vb, grw, gpool, gfox, u, zrw, lf) = _inproj(
        x2d, wl["g_pre"], wl["w_cat"], wl["bf_pad"], 256)
    zeros_last = jnp.zeros((n_seq, 1, RW_SHIFT), F32)
    pre = _rwkv_pre(zrw, zeros_last, wl, 512, t_seq, CHUNK)
    s0 = jnp.zeros((n_seq, HEADS, HEAD_DIM, HEAD_DIM), F32)
    o_rw, s_new = _rwkv_chunk(pre, s0, wl["lnw"], wl["lnb"], n_seq, t_seq, 8)
    o_pool, pool_buf = _pool_prompt(u, wl["pool_w"], wl["pool_scale"], n_seq, t_seq)
    c, ct = _fox_cum(lf, n_seq, t_seq)
    o_fox = _fox_prompt(qb, kb, vb, c, ct, n_seq, t_seq, 256)
    y = _outproj(x2d, o_rw, o_pool, o_fox, grw, gpool, gfox, gm, wl, 512)
    shift = zrw.reshape(n_seq, t_seq, RW_SHIFT)[:, -1]
    state = (s_new, shift, pool_buf,
             k.reshape(n_seq, t_seq, HEADS, HEAD_DIM), v.reshape(n_seq, t_seq, HEADS, HEAD_DIM),
             lf[:, :HEADS].reshape(n_seq, t_seq, HEADS))
    return y, state


def _sample_layer(x2d, wl, layer, n_seq, t_new, st_rwkv, st_shift, st_pool, caches, page_table):
    (gm, qb, k, v, _, _, grw, gpool, gfox, u, zrw, lf) = _inproj(
        x2d, wl["g_pre"], wl["w_cat"], wl["bf_pad"], 256)
    z3 = zrw.reshape(n_seq, t_new, RW_SHIFT)
    z_pad = jnp.pad(z3, ((0, 0), (0, CHUNK - t_new), (0, 0))).reshape(n_seq * CHUNK, RW_SHIFT)
    pre = _rwkv_pre(z_pad, st_shift[:, None, :], wl, CHUNK, CHUNK, t_new)
    o_rw_pad, s_new = _rwkv_chunk(pre, st_rwkv, wl["lnw"], wl["lnb"], n_seq, CHUNK, 1)
    o_rw = o_rw_pad.reshape(n_seq, CHUNK, BRANCH_W)[:, :t_new].reshape(n_seq * t_new, BRANCH_W)
    past = page_table.shape[1] * PAGE
    tm = lambda a: jnp.transpose(a, (1, 0, 2))
    o_pool_tm, pool_buf_tm = _pool_sample(tm(st_pool), tm(u.reshape(n_seq, t_new, BRANCH_W)),
                                          wl["pool_w"], wl["pool_scale"], past)
    o_pool = tm(o_pool_tm).reshape(n_seq * t_new, BRANCH_W)
    cache_k, cache_v, cache_logf_t = caches
    tri = lambda a, w: a.reshape(n_seq, t_new, w)
    lf3 = lf[:, :HEADS].reshape(n_seq, t_new, HEADS)
    o_fox3 = _fox_sample(layer, page_table, tri(qb, BRANCH_W), tri(k, BRANCH_W), tri(v, BRANCH_W),
                         jnp.transpose(lf3, (0, 2, 1)), cache_k, cache_v, cache_logf_t)
    y = _outproj(x2d, o_rw, o_pool, o_fox3.reshape(n_seq * t_new, BRANCH_W), grw, gpool, gfox, gm,
                 wl, 256)
    state = (s_new, z3[:, -1], tm(pool_buf_tm),
             k.reshape(n_seq, t_new, HEADS, HEAD_DIM), v.reshape(n_seq, t_new, HEADS, HEAD_DIM), lf3)
    return y, state


def kernel(x_prompt, x_sample, state_rwkv, state_shift, state_pool, cache_k, cache_v, cache_logf, page_table, norm_pre, norm_post, w_in, b_forget, rw_mu, rw_w0, rw_w2, rw_a0, rw_a2, rw_kk, rw_ka, rw_rk, rw_lnw, rw_lnb, pool_w, pool_scale, proj_a, proj_b, proj_c, w_out):
    depth = w_in.shape[0]
    bp, sp, _ = x_prompt.shape
    bs, ts, _ = x_sample.shape
    params = dict(norm_pre=norm_pre, norm_post=norm_post, b_forget=b_forget, rw_mu=rw_mu, rw_w0=rw_w0,
                  rw_w2=rw_w2, rw_a0=rw_a0, rw_a2=rw_a2, rw_kk=rw_kk, rw_ka=rw_ka, rw_rk=rw_rk,
                  rw_lnw=rw_lnw, rw_lnb=rw_lnb, pool_w=pool_w, pool_scale=pool_scale, proj_a=proj_a,
                  proj_b=proj_b, proj_c=proj_c, w_out=w_out)
    w_cat = _reorder_w_in(w_in)
    n_pool = cache_k.shape[1]
    caches = (cache_k.reshape(depth, n_pool, PAGE, BRANCH_W), cache_v.reshape(depth, n_pool, PAGE, BRANCH_W),
              jnp.transpose(cache_logf, (0, 1, 3, 2)))
    hp = x_prompt.reshape(bp * sp, D_MODEL)
    hs = x_sample.reshape(bs * ts, D_MODEL)
    st_p, st_s = [], []
    for l in range(depth):
        wl = _layer_weights(l, w_cat, params)
        hp, sp_state = _prompt_layer(hp, wl, bp, sp)
        hs, ss_state = _sample_layer(hs, wl, l, bs, ts, state_rwkv[l], state_shift[l], state_pool[l],
                                     caches, page_table)
        st_p.append(sp_state)
        st_s.append(ss_state)
    outs = [hp.reshape(bp, sp, D_MODEL), hs.reshape(bs, ts, D_MODEL)]
    for i in range(6):
        outs.append(jnp.stack([s[i] for s in st_p]))
        outs.append(jnp.stack([s[i] for s in st_s]))
    return tuple(outs)
```

```python
import functools

import jax
import jax.numpy as jnp
from jax import lax
from jax.experimental import pallas as pl
from jax.experimental.pallas import tpu as pltpu

F32 = jnp.float32
BF16 = jnp.bfloat16
HIGHEST = lax.Precision.HIGHEST

D_MODEL = 1024
HEADS = 8
HEAD_DIM = 64
BRANCH_W = HEADS * HEAD_DIM
LORA = 64
RW_SHIFT = 3 * BRANCH_W + 2 * LORA
POOL_WINDOWS = (2, 4, 8, 16)
POOL_GW = 128
POOL_BUF = 15
PAGE = 128
CHUNK = 64
LANES = 128
EPS = 1e-6
LNX_EPS = 64e-5
LOG2E = 1.4426950408889634

C_MERGE = 0
C_QKV = 3 * D_MODEL
C_GRW = C_QKV + 3 * BRANCH_W
C_GPOOL = C_GRW + BRANCH_W
C_GFOX = C_GPOOL + BRANCH_W
C_UPOOL = C_GFOX + BRANCH_W
C_ZRW = C_UPOOL + BRANCH_W
C_F = C_ZRW + RW_SHIFT
W_COLS = C_F + LANES

VMEM_LIMIT = 56 * 1024 * 1024


def _cparams(sem):
    return pltpu.CompilerParams(dimension_semantics=sem, vmem_limit_bytes=VMEM_LIMIT)


def _sigmoid(x):
    return 1.0 / (1.0 + jnp.exp(-x))


def _log_sigmoid(x):
    return jnp.minimum(x, 0.0) - jnp.log1p(jnp.exp(-jnp.abs(x)))


def _dot(a, b):
    return jnp.dot(a.astype(BF16), b.astype(BF16), preferred_element_type=F32)


def _dot_nt(a, b):
    return lax.dot_general(a.astype(BF16), b.astype(BF16), (((1,), (1,)), ((), ())),
                           preferred_element_type=F32)


def _dot_tn(a, b):
    return lax.dot_general(a.astype(BF16), b.astype(BF16), (((0,), (0,)), ((), ())),
                           preferred_element_type=F32)


def _inproj_kernel(x_ref, g_ref, w_ref, bf_ref, gm_ref, q_ref, k_ref, v_ref, kb_ref, vb_ref,
                   grw_ref, gpool_ref, gfox_ref, u_ref, zrw_ref, lf_ref, *, q_scale):
    x = x_ref[...]
    ms = jnp.mean(x * x, axis=-1, keepdims=True)
    xn = (x * lax.rsqrt(ms + EPS) * g_ref[...]).astype(BF16)

    def mm(c0, width):
        return jnp.dot(xn, w_ref[:, c0:c0 + width], preferred_element_type=F32)

    for i in range(3):
        gm_ref[:, i * D_MODEL:(i + 1) * D_MODEL] = _sigmoid(mm(C_MERGE + i * D_MODEL, D_MODEL))
    q_ref[...] = (mm(C_QKV, BRANCH_W) * q_scale).astype(BF16)
    k = mm(C_QKV + BRANCH_W, BRANCH_W)
    k_ref[...] = k
    kb_ref[...] = k.astype(BF16)
    v = mm(C_QKV + 2 * BRANCH_W, BRANCH_W)
    v_ref[...] = v
    vb_ref[...] = v.astype(BF16)
    for ref, c0 in ((grw_ref, C_GRW), (gpool_ref, C_GPOOL), (gfox_ref, C_GFOX)):
        g = mm(c0, BRANCH_W)
        ref[...] = g * _sigmoid(g)
    u_ref[...] = mm(C_UPOOL, BRANCH_W)
    zrw_ref[:, 0:3 * BRANCH_W] = mm(C_ZRW, 3 * BRANCH_W)
    zrw_ref[:, 3 * BRANCH_W:RW_SHIFT] = mm(C_ZRW + 3 * BRANCH_W, 2 * LORA)
    f = mm(C_F, LANES) + bf_ref[...]
    head_lane = lax.broadcasted_iota(jnp.int32, f.shape, 1) < HEADS
    lf_ref[...] = jnp.where(head_lane, _log_sigmoid(f), 0.0)


def _inproj(x2d, g_pre, w_cat, bf_pad, tm, q_scale):
    n = x2d.shape[0]
    row = lambda i: (i, 0)
    const = lambda i: (0, 0)
    widths = [(3 * D_MODEL, F32), (BRANCH_W, BF16), (BRANCH_W, F32), (BRANCH_W, F32),
              (BRANCH_W, BF16), (BRANCH_W, BF16), (BRANCH_W, F32), (BRANCH_W, F32),
              (BRANCH_W, F32), (BRANCH_W, F32), (RW_SHIFT, F32), (LANES, F32)]
    return pl.pallas_call(
        functools.partial(_inproj_kernel, q_scale=q_scale),
        grid=(n // tm,),
        in_specs=[pl.BlockSpec((tm, D_MODEL), row),
                  pl.BlockSpec((1, D_MODEL), const),
                  pl.BlockSpec((D_MODEL, W_COLS), const, pipeline_mode=pl.Buffered(1)),
                  pl.BlockSpec((1, LANES), const)],
        out_specs=[pl.BlockSpec((tm, w), row) for w, _ in widths],
        out_shape=[jax.ShapeDtypeStruct((n, w), dt) for w, dt in widths],
        compiler_params=_cparams(("parallel",)),
        name="inproj",
    )(x2d, g_pre, w_cat, bf_pad)


def _seg_sum(x, lo_mask):
    s_lo = jnp.sum(jnp.where(lo_mask, x, 0.0), axis=-1, keepdims=True)
    s_hi = jnp.sum(jnp.where(lo_mask, 0.0, x), axis=-1, keepdims=True)
    return jnp.where(lo_mask, s_lo, s_hi)


def _seg_sum_wide(x):
    tt = x.shape[0]
    lo_mask = lax.broadcasted_iota(jnp.int32, (tt, LANES), 1) < HEAD_DIM
    parts = [_seg_sum(x[:, p * LANES:(p + 1) * LANES], lo_mask) for p in range(x.shape[1] // LANES)]
    return jnp.concatenate(parts, axis=-1)


def _rwkv_pre_kernel(z_ref, zp_ref, zl_ref, mu_ref, w0_ref, w2_ref, a0_ref, a2_ref, kk_ref, ka_ref,
                     rk_ref, at_ref, rt_ref, bt_ref, kt_ref, vb_ref, bonus_ref, pc_ref,
                     *, tt, t_seq):
    i = pl.program_id(0)
    z = z_ref[...]
    starts_seq = (i * tt) % t_seq == 0
    prev_row = jnp.where(starts_seq, zl_ref[...], zp_ref[7:8, :])
    row = lax.broadcasted_iota(jnp.int32, (tt, 1), 0)
    z_prev = jnp.where(row == 0, prev_row, pltpu.roll(z, 1, axis=0))
    zs = z + (z_prev - z) * mu_ref[...]
    w3 = BRANCH_W
    r = zs[:, 0:w3]
    k = zs[:, w3:2 * w3]
    v = zs[:, 2 * w3:3 * w3]
    lora = zs[:, 3 * w3:RW_SHIFT]
    wx = w0_ref[...] + jnp.dot(jnp.tanh(lora).astype(BF16), w2_ref[...], preferred_element_type=F32)
    logd = -jnp.exp(_log_sigmoid(wx) - 0.5)
    a = _sigmoid(a0_ref[...] + jnp.dot(lora.astype(BF16), a2_ref[...], preferred_element_type=F32))
    kk = k * kk_ref[...]
    k2 = k * (1.0 + (a - 1.0) * ka_ref[...])
    kk = kk / jnp.maximum(jnp.sqrt(_seg_sum_wide(kk * kk)), 1e-12)
    bonus_ref[...] = _seg_sum_wide(r * k2 * rk_ref[...]) * v
    ri = lax.broadcasted_iota(jnp.int32, (CHUNK, CHUNK), 0)
    ci = lax.broadcasted_iota(jnp.int32, (CHUNK, CHUNK), 1)
    tri = jnp.where(ci <= ri, 1.0, 0.0).astype(F32)
    cl = jnp.concatenate(
        [jnp.dot(tri, logd[c * CHUNK:(c + 1) * CHUNK, :], preferred_element_type=F32, precision=HIGHEST)
         for c in range(tt // CHUNK)], axis=0)
    e_neg = jnp.exp(-cl)
    at_ref[...] = (-kk * jnp.exp(cl - logd)).astype(BF16)
    rt_ref[...] = (r * jnp.exp(cl)).astype(BF16)
    bt_ref[...] = (kk * a * e_neg).astype(BF16)
    kt_ref[...] = (k2 * e_neg).astype(BF16)
    vb_ref[...] = v.astype(BF16)
    for c in range(tt // CHUNK):
        last = cl[(c + 1) * CHUNK - 1:(c + 1) * CHUNK, :]
        pc_ref[c * 8:(c + 1) * 8, :] = jnp.broadcast_to(jnp.exp(last), (8, BRANCH_W))


def _rwkv_pre(z, z_last, wl, tt, t_seq):
    n = z.shape[0]
    row = lambda i: (i, 0)
    const = lambda i: (0, 0)
    vec = lambda w: pl.BlockSpec((1, w), const)
    nb8 = tt // 8
    kern = functools.partial(_rwkv_pre_kernel, tt=tt, t_seq=t_seq)
    wide = lambda dt: jax.ShapeDtypeStruct((n, BRANCH_W), dt)
    return pl.pallas_call(
        kern,
        grid=(n // tt,),
        in_specs=[pl.BlockSpec((tt, RW_SHIFT), row),
                  pl.BlockSpec((8, RW_SHIFT), lambda i: (jnp.maximum(i * nb8 - 1, 0), 0)),
                  pl.BlockSpec((None, 1, RW_SHIFT), lambda i: ((i * tt) // t_seq, 0, 0)),
                  vec(RW_SHIFT), vec(BRANCH_W),
                  pl.BlockSpec((LANES, BRANCH_W), const),
                  vec(BRANCH_W),
                  pl.BlockSpec((LANES, BRANCH_W), const),
                  vec(BRANCH_W), vec(BRANCH_W), vec(BRANCH_W)],
        out_specs=[pl.BlockSpec((tt, BRANCH_W), row)] * 6
                  + [pl.BlockSpec((tt // CHUNK * 8, BRANCH_W), row)],
        out_shape=[wide(BF16), wide(BF16), wide(BF16), wide(BF16), wide(BF16), wide(F32),
                   jax.ShapeDtypeStruct((n // CHUNK * 8, BRANCH_W), F32)],
        compiler_params=_cparams(("parallel",)),
        name="rwkv_pre",
    )(z, z, z_last, wl["mu"], wl["w0"], wl["w2p"], wl["a0"], wl["a2p"], wl["kk"], wl["ka"], wl["rk"])


def _rwkv_chunk_kernel(at_ref, rt_ref, bt_ref, kt_ref, vb_ref, bonus_ref, pc_ref, lnw_ref,
                       lnb_ref, o_ref, sout_ref, s_scr, *, cpb):
    g = pl.program_id(1)
    c2 = 2 * CHUNK
    pairs = HEADS // 2

    @pl.when(g == 0)
    def _():
        s_scr[...] = jnp.zeros(s_scr.shape, F32)

    lane = lax.broadcasted_iota(jnp.int32, (CHUNK, LANES), 1)
    lo = lane < HEAD_DIM
    ri = lax.broadcasted_iota(jnp.int32, (c2, c2), 0)
    ci = lax.broadcasted_iota(jnp.int32, (c2, c2), 1)
    same = (ri // CHUNK) == (ci // CHUNK)
    strict = same & (ci < ri)
    incl = same & (ci <= ri)
    eye = jnp.where(ri == ci, 1.0, 0.0).astype(F32)

    def stack2(x):
        x = x.astype(F32)
        return jnp.concatenate([jnp.where(lo, x, 0.0), jnp.where(lo, 0.0, x)], axis=0)

    def body(c, carry):
        rows = pl.ds(pl.multiple_of(c * CHUNK, CHUNK), CHUNK)
        pc_row = pl.ds(pl.multiple_of(c * 8, 8), 1)
        lanes = [slice(p * LANES, (p + 1) * LANES) for p in range(pairs)]
        each = lambda f, *xs: [f(*args) for args in zip(*xs)]
        a_s = [stack2(at_ref[rows, ln]) for ln in lanes]
        r_s = [stack2(rt_ref[rows, ln]) for ln in lanes]
        b_s = [stack2(bt_ref[rows, ln]) for ln in lanes]
        k_s = [stack2(kt_ref[rows, ln]) for ln in lanes]
        vp = [vb_ref[rows, ln].astype(F32) for ln in lanes]
        v_s = [jnp.concatenate([x[:, :HEAD_DIM], x[:, HEAD_DIM:]], axis=0) for x in vp]
        pc = [pc_ref[pc_row, ln] for ln in lanes]
        nmat = each(lambda a, b: jnp.where(strict, _dot_nt(a, b), 0.0), a_s, b_s)
        lak = each(lambda a, k: jnp.where(strict, _dot_nt(a, k), 0.0), a_s, k_s)
        mrb = each(lambda r, b: jnp.where(incl, _dot_nt(r, b), 0.0), r_s, b_s)
        mrk = each(lambda r, k: jnp.where(incl, _dot_nt(r, k), 0.0), r_s, k_s)
        lakv = each(_dot, lak, v_s)
        mrkv = each(_dot, mrk, v_s)
        vtk = each(_dot_tn, v_s, k_s)
        tinv = [eye + n for n in nmat]
        pw = nmat
        for _ in range(5):
            pw = each(_dot, pw, pw)
            tinv = each(lambda t, q: t + _dot(t, q), tinv, pw)
        a_hat = each(_dot, tinv, a_s)
        u_hat = each(_dot, tinv, lakv)
        r_hat = each(lambda r, m, a: r + _dot(m, a), r_s, mrb, a_hat)
        y_hat = each(lambda m, u, y: _dot(m, u) + y, mrb, u_hat, mrkv)
        gmat = each(lambda a, b, x: (eye + _dot_tn(a, b)) * x, a_hat, b_s, pc)
        hmat = each(lambda u, b, y, x: (_dot_tn(u, b) + y) * x, u_hat, b_s, vtk, pc)
        s = [s_scr[p] for p in range(pairs)]
        y_s = each(lambda r, st, y: _dot_nt(r, st) + y, r_hat, s, y_hat)
        s_new = each(lambda st, gm, hm: _dot(st, gm) + hm, s, gmat, hmat)
        for p in range(pairs):
            s_scr[p] = s_new[p]
            y = jnp.concatenate([y_s[p][:CHUNK], y_s[p][CHUNK:]], axis=1)
            mean = _seg_sum(y, lo) * (1.0 / HEAD_DIM)
            yc = y - mean
            var = _seg_sum(yc * yc, lo) * (1.0 / HEAD_DIM)
            ln = lanes[p]
            o_ref[rows, ln] = (yc * lax.rsqrt(var + LNX_EPS) * lnw_ref[:, ln] + lnb_ref[:, ln]
                               + bonus_ref[rows, ln])
        return carry

    lax.fori_loop(0, cpb, body, 0)

    @pl.when(g == pl.num_programs(1) - 1)
    def _():
        for p in range(pairs):
            s = s_scr[p]
            sout_ref[2 * p] = s[:, :HEAD_DIM]
            sout_ref[2 * p + 1] = s[:, HEAD_DIM:]


def _rwkv_chunk(pre, lnw, lnb, n_seq, t_seq, cpb):
    at, rt, bt, kt, vb, bonus, pc = pre
    n = at.shape[0]
    ng = t_seq // (cpb * CHUNK)
    rows = cpb * CHUNK
    tile = lambda b, g: (b * ng + g, 0)
    const = lambda b, g: (0, 0)
    state = pl.BlockSpec((None, HEADS, HEAD_DIM, HEAD_DIM), lambda b, g: (b, 0, 0, 0))
    kern = functools.partial(_rwkv_chunk_kernel, cpb=cpb)
    return pl.pallas_call(
        kern,
        grid=(n_seq, ng),
        in_specs=[pl.BlockSpec((rows, BRANCH_W), tile)] * 6
                 + [pl.BlockSpec((cpb * 8, BRANCH_W), tile),
                    pl.BlockSpec((1, BRANCH_W), const),
                    pl.BlockSpec((1, BRANCH_W), const)],
        out_specs=[pl.BlockSpec((rows, BRANCH_W), tile), state],
        out_shape=[jax.ShapeDtypeStruct((n, BRANCH_W), F32),
                   jax.ShapeDtypeStruct((n_seq, HEADS, HEAD_DIM, HEAD_DIM), F32)],
        scratch_shapes=[pltpu.VMEM((HEADS // 2, HEAD_DIM, LANES), F32)],
        compiler_params=_cparams(("parallel", "arbitrary")),
        name="rwkv_chunk",
    )(at, rt, bt, kt, vb, bonus, pc, lnw, lnb)


def _rwkv_sample_kernel(zr_ref, zk_ref, zv_ref, zl_ref, lr_ref, lk_ref, lv_ref, ll_ref,
                        mur_ref, muk_ref, muv_ref, mul_ref, w0_ref, w2_ref, a0_ref, a2_ref, kk_ref,
                        ka_ref, rk_ref, lnw_ref, lnb_ref, s_ref, o_ref, sout_ref, tr_ref, yt_ref,
                        *, t_new):
    n_b = zr_ref.shape[1]
    lo = lax.broadcasted_iota(jnp.int32, (n_b, LANES), 1) < HEAD_DIM

    def shifted(cur_ref, last_ref, mu_ref, t):
        cur = cur_ref[t]
        prev = last_ref[...] if t == 0 else cur_ref[t - 1]
        return cur + (prev - cur) * mu_ref[...]

    for t in range(t_new):
        r = shifted(zr_ref, lr_ref, mur_ref, t)
        k = shifted(zk_ref, lk_ref, muk_ref, t)
        v = shifted(zv_ref, lv_ref, muv_ref, t)
        lora = shifted(zl_ref, ll_ref, mul_ref, t)
        wx = w0_ref[...] + jnp.dot(jnp.tanh(lora).astype(BF16), w2_ref[...], preferred_element_type=F32)
        logd = -jnp.exp(_log_sigmoid(wx) - 0.5)
        a = _sigmoid(a0_ref[...] + jnp.dot(lora.astype(BF16), a2_ref[...], preferred_element_type=F32))
        kk = k * kk_ref[...]
        k2 = k * (1.0 + (a - 1.0) * ka_ref[...])
        kk = kk / jnp.maximum(jnp.sqrt(_seg_sum(kk * kk, lo)), 1e-12)
        o_ref[t] = _seg_sum(r * k2 * rk_ref[...], lo) * v
        for i, x in enumerate((-kk, jnp.exp(logd), kk * a, k2, r, v)):
            tr_ref[t, i] = x.T

    for hh in range(2):
        h0 = hh * HEAD_DIM

        def row_step(vi, carry):
            s_v = s_ref[hh, vi]
            for t in range(t_new):
                nkk = tr_ref[t, 0, h0:h0 + HEAD_DIM, :]
                dec = tr_ref[t, 1, h0:h0 + HEAD_DIM, :]
                kka = tr_ref[t, 2, h0:h0 + HEAD_DIM, :]
                k2 = tr_ref[t, 3, h0:h0 + HEAD_DIM, :]
                r = tr_ref[t, 4, h0:h0 + HEAD_DIM, :]
                vv = tr_ref[t, 5, pl.ds(h0 + vi, 1), :]
                sa = jnp.sum(s_v * nkk, axis=0, keepdims=True)
                s_v = s_v * dec + sa * kka + vv * k2
                yt_ref[t, pl.ds(h0 + vi, 1), :] = jnp.sum(s_v * r, axis=0, keepdims=True)
            sout_ref[hh, vi] = s_v
            return carry

        lax.fori_loop(0, HEAD_DIM, row_step, 0, unroll=4)

    for t in range(t_new):
        y = yt_ref[t].T
        mean = _seg_sum(y, lo) * (1.0 / HEAD_DIM)
        yc = y - mean
        var = _seg_sum(yc * yc, lo) * (1.0 / HEAD_DIM)
        o_ref[t] = o_ref[t] + yc * lax.rsqrt(var + LNX_EPS) * lnw_ref[...] + lnb_ref[...]


def _rwkv_sample(layer, zt, z_last, wl, state_bl):
    t_new, n_b, _ = zt.shape
    pairs = HEADS // 2
    lora_tile = 3 * pairs
    seg = lambda s: pl.BlockSpec((t_new, n_b, LANES), lambda p, s=s: (0, 0, s * pairs + p))
    seg_last = lambda s: pl.BlockSpec((n_b, LANES), lambda p, s=s: (0, s * pairs + p))
    seg_mu = lambda s: pl.BlockSpec((1, LANES), lambda p, s=s: (0, s * pairs + p))
    vec = pl.BlockSpec((1, LANES), lambda p: (0, p))
    lora_w = pl.BlockSpec((LANES, LANES), lambda p: (0, p))
    kern = functools.partial(_rwkv_sample_kernel, t_new=t_new)
    return pl.pallas_call(
        kern,
        grid=(pairs,),
        in_specs=[seg(0), seg(1), seg(2),
                  pl.BlockSpec((t_new, n_b, LANES), lambda p: (0, 0, lora_tile)),
                  seg_last(0), seg_last(1), seg_last(2),
                  pl.BlockSpec((n_b, LANES), lambda p: (0, lora_tile)),
                  seg_mu(0), seg_mu(1), seg_mu(2),
                  pl.BlockSpec((1, LANES), lambda p: (0, lora_tile)),
                  vec, lora_w, vec, lora_w, vec, vec, vec, vec, vec,
                  pl.BlockSpec((None, 2, HEAD_DIM, HEAD_DIM, n_b), lambda p: (layer, p, 0, 0, 0))],
        out_specs=[pl.BlockSpec((t_new, n_b, LANES), lambda p: (0, 0, p)),
                   pl.BlockSpec((2, HEAD_DIM, HEAD_DIM, n_b), lambda p: (p, 0, 0, 0))],
        out_shape=[jax.ShapeDtypeStruct((t_new, n_b, BRANCH_W), F32),
                   jax.ShapeDtypeStruct((HEADS, HEAD_DIM, HEAD_DIM, n_b), F32)],
        scratch_shapes=[pltpu.VMEM((t_new, 6, LANES, n_b), F32), pltpu.VMEM((t_new, LANES, n_b), F32)],
        compiler_params=_cparams(("parallel",)),
        name="rwkv_sample",
    )(zt, zt, zt, zt, z_last, z_last, z_last, z_last, wl["mu"], wl["mu"], wl["mu"], wl["mu"],
      wl["w0"], wl["w2p"], wl["a0"], wl["a2p"], wl["kk"], wl["ka"], wl["rk"], wl["lnw"], wl["lnb"],
      state_bl)


def _pool_prompt_kernel(u_ref, pw_ref, ps_ref, o_ref, buf_ref, ext_ref, *, t_seq, tile):
    pad = POOL_BUF + 1
    ext_ref[0:pad, :] = jnp.zeros((pad, BRANCH_W), F32)
    ext_ref[pad:pad + t_seq, :] = u_ref[...]
    buf_ref[...] = u_ref[t_seq - POOL_BUF:t_seq, :]
    for t0 in range(0, t_seq, tile):
        pos = t0 + lax.broadcasted_iota(jnp.int32, (tile, 1), 0)
        for gi, w in enumerate(POOL_WINDOWS):
            sl = slice(gi * POOL_GW, (gi + 1) * POOL_GW)
            wsum = ext_ref[pad + t0:pad + t0 + tile, sl]
            for j in range(1, w):
                wsum = wsum + ext_ref[pad + t0 - j:pad + t0 - j + tile, sl]
            cnt = jnp.minimum(pos + 1, w).astype(F32)
            d = wsum / cnt - ext_ref[pad + t0:pad + t0 + tile, sl]
            o_ref[t0:t0 + tile, sl] = _dot(d, pw_ref[gi]) * ps_ref[:, sl]


def _pool_prompt(u, pool_w, pool_scale, n_seq, t_seq):
    kern = functools.partial(_pool_prompt_kernel, t_seq=t_seq, tile=256)
    return pl.pallas_call(
        kern,
        grid=(n_seq,),
        in_specs=[pl.BlockSpec((t_seq, BRANCH_W), lambda b: (b, 0)),
                  pl.BlockSpec((len(POOL_WINDOWS), POOL_GW, POOL_GW), lambda b: (0, 0, 0)),
                  pl.BlockSpec((1, BRANCH_W), lambda b: (0, 0))],
        out_specs=[pl.BlockSpec((t_seq, BRANCH_W), lambda b: (b, 0)),
                   pl.BlockSpec((None, POOL_BUF, BRANCH_W), lambda b: (b, 0, 0))],
        out_shape=[jax.ShapeDtypeStruct((n_seq * t_seq, BRANCH_W), F32),
                   jax.ShapeDtypeStruct((n_seq, POOL_BUF, BRANCH_W), F32)],
        scratch_shapes=[pltpu.VMEM((POOL_BUF + 1 + t_seq, BRANCH_W), F32)],
        compiler_params=_cparams(("parallel",)),
        name="pool_prompt",
    )(u, pool_w, pool_scale)


def _pool_sample_kernel(pre_ref, u_ref, pw_ref, ps_ref, o_ref, buf_ref, *, t_new, pos0):
    def ext_row(i, sl):
        if i < POOL_BUF:
            return pre_ref[i, :, sl]
        return u_ref[i - POOL_BUF, :, sl]

    full = slice(0, BRANCH_W)
    for i in range(POOL_BUF):
        buf_ref[i] = ext_row(i + t_new, full)
    for t in range(t_new):
        for gi, w in enumerate(POOL_WINDOWS):
            sl = slice(gi * POOL_GW, (gi + 1) * POOL_GW)
            cur = ext_row(POOL_BUF + t, sl)
            wsum = cur
            for j in range(1, w):
                wsum = wsum + ext_row(POOL_BUF + t - j, sl)
            cnt = float(min(pos0 + t + 1, w))
            d = wsum / cnt - cur
            o_ref[t, :, sl] = _dot(d, pw_ref[gi]) * ps_ref[:, sl]


def _pool_sample(prefix, u3, pool_w, pool_scale, pos0):
    t_new = u3.shape[0]
    kern = functools.partial(_pool_sample_kernel, t_new=t_new, pos0=pos0)
    full3 = lambda shape: pl.BlockSpec(shape, lambda i: (0, 0, 0))
    return pl.pallas_call(
        kern,
        grid=(1,),
        in_specs=[full3(prefix.shape), full3(u3.shape), full3(pool_w.shape),
                  pl.BlockSpec((1, BRANCH_W), lambda i: (0, 0))],
        out_specs=[full3(u3.shape), full3(prefix.shape)],
        out_shape=[jax.ShapeDtypeStruct(u3.shape, F32), jax.ShapeDtypeStruct(prefix.shape, F32)],
        compiler_params=_cparams(("arbitrary",)),
        name="pool_sample",
    )(prefix, u3, pool_w, pool_scale)


def _fox_cum_kernel(lf_ref, c_ref, ct_ref, *, t_seq, tile):
    ri = lax.broadcasted_iota(jnp.int32, (tile, tile), 0)
    ci = lax.broadcasted_iota(jnp.int32, (tile, tile), 1)
    tri = jnp.where(ci <= ri, 1.0, 0.0).astype(F32)
    e8 = jnp.where(lax.broadcasted_iota(jnp.int32, (HEADS, LANES), 0)
                   == lax.broadcasted_iota(jnp.int32, (HEADS, LANES), 1), 1.0, 0.0).astype(F32)
    carry = jnp.zeros((1, LANES), F32)
    for t0 in range(0, t_seq, tile):
        c = jnp.dot(tri, lf_ref[t0:t0 + tile, :], preferred_element_type=F32, precision=HIGHEST) + carry
        c_ref[t0:t0 + tile, :] = c
        ct_ref[:, t0:t0 + tile] = lax.dot_general(e8, c, (((1,), (1,)), ((), ())),
                                                  preferred_element_type=F32, precision=HIGHEST)
        carry = c[tile - 1:tile, :]


def _fox_cum(lf, n_seq, t_seq):
    kern = functools.partial(_fox_cum_kernel, t_seq=t_seq, tile=256)
    return pl.pallas_call(
        kern,
        grid=(n_seq,),
        in_specs=[pl.BlockSpec((t_seq, LANES), lambda b: (b, 0))],
        out_specs=[pl.BlockSpec((t_seq, LANES), lambda b: (b, 0)),
                   pl.BlockSpec((None, HEADS, t_seq), lambda b: (b, 0, 0))],
        out_shape=[jax.ShapeDtypeStruct((n_seq * t_seq, LANES), F32),
                   jax.ShapeDtypeStruct((n_seq, HEADS, t_seq), F32)],
        compiler_params=_cparams(("parallel",)),
        name="fox_cum",
    )(lf)


def _fox_prompt_kernel(q_ref, k_ref, v_ref, c_ref, ct_ref, o_ref, *, tq):
    p = pl.program_id(1)
    qi = pl.program_id(2)
    q = q_ref[...].astype(F32)
    lane = lax.broadcasted_iota(jnp.int32, (tq, LANES), 1)
    lo = lane < HEAD_DIM
    ri = lax.broadcasted_iota(jnp.int32, (tq, tq), 0)
    ci = lax.broadcasted_iota(jnp.int32, (tq, tq), 1)
    causal = ci <= ri
    c_all = c_ref[...]
    qh = [jnp.where(lo, q, 0.0).astype(BF16), jnp.where(lo, 0.0, q).astype(BF16)]
    cq = [jnp.sum(jnp.where(lane == 2 * p + hh, c_all, 0.0), axis=-1, keepdims=True) * LOG2E
          for hh in range(2)]

    def block(j, carry, masked):
        cols = pl.ds(pl.multiple_of(j * tq, tq), tq)
        k = k_ref[cols, :]
        v = v_ref[cols, :]
        heads = range(2)
        ck = [ct_ref[pl.ds(2 * p + hh, 1), cols] * LOG2E for hh in heads]
        s = [lax.dot_general(qh[hh], k, (((1,), (1,)), ((), ())), preferred_element_type=F32) - ck[hh]
             for hh in heads]
        if masked:
            s = [jnp.where(causal, x, -jnp.inf) for x in s]
        m_new = [jnp.maximum(carry[hh][0], jnp.max(s[hh], axis=-1, keepdims=True) + cq[hh]) for hh in heads]
        pr = [jnp.exp2(s[hh] - (m_new[hh] - cq[hh])) for hh in heads]
        alpha = [jnp.exp2(carry[hh][0] - m_new[hh]) for hh in heads]
        l = [alpha[hh] * carry[hh][1] + jnp.sum(pr[hh], axis=-1, keepdims=True) for hh in heads]
        acc = [alpha[hh] * carry[hh][2] + jnp.dot(pr[hh].astype(BF16), v, preferred_element_type=F32)
               for hh in heads]
        return tuple((m_new[hh], l[hh], acc[hh]) for hh in heads)

    one = (jnp.full((tq, 1), -jnp.inf, F32), jnp.zeros((tq, 1), F32), jnp.zeros((tq, LANES), F32))
    carry = lax.fori_loop(0, qi, lambda j, c: block(j, c, False), (one, one))
    (_, l0, acc0), (_, l1, acc1) = block(qi, carry, True)
    o_ref[...] = jnp.where(lo, acc0 / l0, acc1 / l1)


def _fox_prompt(qb, kb, vb, c, ct, n_seq, t_seq, tq):
    nq = t_seq // tq
    pairs = HEADS // 2
    kern = functools.partial(_fox_prompt_kernel, tq=tq)
    return pl.pallas_call(
        kern,
        grid=(n_seq, pairs, nq),
        in_specs=[pl.BlockSpec((tq, LANES), lambda b, p, i: (b * nq + i, p)),
                  pl.BlockSpec((t_seq, LANES), lambda b, p, i: (b, p)),
                  pl.BlockSpec((t_seq, LANES), lambda b, p, i: (b, p)),
                  pl.BlockSpec((tq, LANES), lambda b, p, i: (b * nq + i, 0)),
                  pl.BlockSpec((None, HEADS, t_seq), lambda b, p, i: (b, 0, 0))],
        out_specs=pl.BlockSpec((tq, LANES), lambda b, p, i: (b * nq + i, p)),
        out_shape=jax.ShapeDtypeStruct((n_seq * t_seq, BRANCH_W), F32),
        compiler_params=_cparams(("parallel", "parallel", "arbitrary")),
        name="fox_prompt",
    )(qb, kb, vb, c, ct)


def _fox_sample_kernel(pt_ref, q_ref, kn_ref, vn_ref, lfn_ref, *refs, n_pages, t_new):
    del pt_ref
    k_pages = refs[0:n_pages]
    v_pages = refs[n_pages:2 * n_pages]
    lf_pages = refs[2 * n_pages:3 * n_pages]
    o_ref = refs[3 * n_pages]
    past = n_pages * PAGE
    rows = t_new * HEADS
    nt = (((1,), (1,)), ((), ()))
    lf_t = jnp.concatenate([lf_pages[i][...] for i in range(n_pages)], axis=1)
    pos = lax.broadcasted_iota(jnp.int32, (HEADS, past), 1)
    x = lf_t
    sh = 1
    while sh < past:
        x = x + jnp.where(pos + sh < past, pltpu.roll(x, past - sh, axis=1), 0.0)
        sh *= 2
    suffix = x - lf_t
    lfn = lfn_ref[...]
    cn = [lfn[:, 0:1]]
    for t in range(1, t_new):
        cn.append(cn[-1] + lfn[:, t:t + 1])
    q = q_ref[...].astype(F32)
    head_of_lane = lax.broadcasted_iota(jnp.int32, (HEADS, BRANCH_W), 1) // HEAD_DIM
    head_mask = head_of_lane == lax.broadcasted_iota(jnp.int32, (HEADS, BRANCH_W), 0)
    qbd = jnp.concatenate(
        [jnp.where(head_mask, jnp.broadcast_to(q[t:t + 1, :], (HEADS, BRANCH_W)), 0.0)
         for t in range(t_new)], axis=0)
    qbd_bf = qbd.astype(BF16)
    s_past = jnp.concatenate(
        [jnp.dot(qbd_bf, k_pages[i][...].astype(BF16), preferred_element_type=F32)
         for i in range(n_pages)], axis=1)
    s_past = s_past + jnp.concatenate([suffix + cn[t] for t in range(t_new)], axis=0)
    row_t = lax.broadcasted_iota(jnp.int32, (rows, 1), 0) // HEADS
    kn = kn_ref[...]
    vn = vn_ref[...]
    s_new = []
    for j in range(t_new):
        dots = jnp.sum(qbd * kn[j:j + 1, :], axis=-1, keepdims=True)
        bias = jnp.concatenate([cn[t] - cn[j] for t in range(t_new)], axis=0)
        s_new.append(jnp.where(row_t >= j, dots + bias, -jnp.inf))
    m = jnp.max(s_past, axis=-1, keepdims=True)
    for j in range(t_new):
        m = jnp.maximum(m, s_new[j])
    p_past = jnp.exp(s_past - m)
    l = jnp.sum(p_past, axis=-1, keepdims=True)
    o = jnp.zeros((rows, BRANCH_W), F32)
    for j in range(t_new):
        p_j = jnp.exp(s_new[j] - m)
        l = l + p_j
        o = o + p_j * vn[j:j + 1, :]
    for i in range(n_pages):
        o = o + lax.dot_general(p_past[:, i * PAGE:(i + 1) * PAGE].astype(BF16),
                                v_pages[i][...].astype(BF16), nt, preferred_element_type=F32)
    o = o / l
    for t in range(t_new):
        blk = jnp.where(head_mask, o[t * HEADS:(t + 1) * HEADS, :], 0.0)
        o_ref[t:t + 1, :] = jnp.sum(blk, axis=0, keepdims=True)


def _fox_sample(layer, page_table, q3, k3, v3, lf3t, cache_k, cache_v, cache_logf_t):
    n_seq, t_new, _ = q3.shape
    n_pages = page_table.shape[1]
    kern = functools.partial(_fox_sample_kernel, n_pages=n_pages, t_new=t_new)
    new_spec = lambda w: pl.BlockSpec((None, t_new, w), lambda b, pt: (b, 0, 0))

    def page_spec(i, rows, w):
        return pl.BlockSpec((None, None, rows, w), lambda b, pt: (layer, pt[b, i], 0, 0))

    grid_spec = pltpu.PrefetchScalarGridSpec(
        num_scalar_prefetch=1,
        grid=(n_seq,),
        in_specs=[new_spec(BRANCH_W), new_spec(BRANCH_W), new_spec(BRANCH_W),
                  pl.BlockSpec((None, HEADS, t_new), lambda b, pt: (b, 0, 0))]
                 + [page_spec(i, BRANCH_W, PAGE) for i in range(n_pages)]
                 + [page_spec(i, BRANCH_W, PAGE) for i in range(n_pages)]
                 + [page_spec(i, HEADS, PAGE) for i in range(n_pages)],
        out_specs=pl.BlockSpec((None, t_new, BRANCH_W), lambda b, pt: (b, 0, 0)),
    )
    return pl.pallas_call(
        kern,
        grid_spec=grid_spec,
        out_shape=jax.ShapeDtypeStruct((n_seq, t_new, BRANCH_W), F32),
        compiler_params=_cparams(("parallel",)),
        name="fox_sample",
    )(page_table, q3, k3, v3, lf3t, *([cache_k] * n_pages), *([cache_v] * n_pages),
      *([cache_logf_t] * n_pages))


def _outproj_kernel(x_ref, oa_ref, ob_ref, oc_ref, ga_ref, gb_ref, gc_ref, gm_ref, pa_ref, pb_ref,
                    pc_ref, wo_ref, gp_ref, y_ref):
    ya = _dot(oa_ref[...] * ga_ref[...], pa_ref[...])
    yb = _dot(ob_ref[...] * gb_ref[...], pb_ref[...])
    yc = _dot(oc_ref[...] * gc_ref[...], pc_ref[...])
    m = (gm_ref[:, 0:D_MODEL] * ya + gm_ref[:, D_MODEL:2 * D_MODEL] * yb
         + gm_ref[:, 2 * D_MODEL:3 * D_MODEL] * yc)
    y = _dot(m, wo_ref[...])
    ms = jnp.mean(y * y, axis=-1, keepdims=True)
    y_ref[...] = x_ref[...] + y * lax.rsqrt(ms + EPS) * gp_ref[...]


def _outproj(x2d, oa, ob, oc, ga, gb, gc, gm, wl, tm):
    n = x2d.shape[0]
    row = lambda i: (i, 0)
    const = lambda i: (0, 0)
    wide = pl.BlockSpec((tm, BRANCH_W), row)
    proj = pl.BlockSpec((BRANCH_W, D_MODEL), const)
    return pl.pallas_call(
        _outproj_kernel,
        grid=(n // tm,),
        in_specs=[pl.BlockSpec((tm, D_MODEL), row), wide, wide, wide, wide, wide, wide,
                  pl.BlockSpec((tm, 3 * D_MODEL), row), proj, proj, proj,
                  pl.BlockSpec((D_MODEL, D_MODEL), const), pl.BlockSpec((1, D_MODEL), const)],
        out_specs=pl.BlockSpec((tm, D_MODEL), row),
        out_shape=jax.ShapeDtypeStruct((n, D_MODEL), F32),
        compiler_params=_cparams(("parallel",)),
        name="outproj",
    )(x2d, oa, ob, oc, ga, gb, gc, gm, wl["proj_a"], wl["proj_b"], wl["proj_c"], wl["w_out"],
      wl["g_post"])


def _layer_weights(l, w_cat, p):
    row = lambda a: a[l][None, :]
    zeros = jnp.zeros((LORA, BRANCH_W), BF16)
    return {
        "w_cat": w_cat[l],
        "g_pre": row(p["norm_pre"]), "g_post": row(p["norm_post"]),
        "bf_pad": jnp.pad(p["b_forget"][l], (0, LANES - HEADS))[None, :],
        "mu": row(p["rw_mu"]), "w0": row(p["rw_w0"]), "a0": row(p["rw_a0"]),
        "w2p": jnp.concatenate([p["rw_w2"][l].astype(BF16), zeros], axis=0),
        "a2p": jnp.concatenate([zeros, p["rw_a2"][l].astype(BF16)], axis=0),
        "kk": row(p["rw_kk"]), "ka": row(p["rw_ka"]), "rk": row(p["rw_rk"]),
        "lnw": row(p["rw_lnw"]), "lnb": row(p["rw_lnb"]),
        "pool_w": p["pool_w"][l].astype(BF16), "pool_scale": row(p["pool_scale"]),
        "proj_a": p["proj_a"][l].astype(BF16), "proj_b": p["proj_b"][l].astype(BF16),
        "proj_c": p["proj_c"][l].astype(BF16), "w_out": p["w_out"][l].astype(BF16),
    }


def _reorder_w_in(w_in):
    a = RW_SHIFT
    segs = {"z_rw": (0, a), "g_rw": (a, a + 512), "u": (a + 512, a + 1024), "g_pool": (a + 1024, a + 1536),
            "qkv": (a + 1536, a + 3072), "f": (a + 3072, a + 3080), "g_fox": (a + 3080, a + 3592),
            "merge": (a + 3592, a + 3592 + 3 * D_MODEL)}
    order = ["merge", "qkv", "g_rw", "g_pool", "g_fox", "u", "z_rw", "f"]
    cols = [w_in[..., segs[s][0]:segs[s][1]] for s in order]
    cols.append(jnp.zeros(w_in.shape[:-1] + (LANES - HEADS,), w_in.dtype))
    return jnp.concatenate(cols, axis=-1).astype(BF16)


def _prompt_layer(x2d, wl, n_seq, t_seq):
    (gm, qb, k, v, kb, vb, grw, gpool, gfox, u, zrw, lf) = _inproj(
        x2d, wl["g_pre"], wl["w_cat"], wl["bf_pad"], 256, HEAD_DIM ** -0.5 * LOG2E)
    zeros_last = jnp.zeros((n_seq, 1, RW_SHIFT), F32)
    pre = _rwkv_pre(zrw, zeros_last, wl, 512, t_seq)
    o_rw, s_new = _rwkv_chunk(pre, wl["lnw"], wl["lnb"], n_seq, t_seq, 8)
    o_pool, pool_buf = _pool_prompt(u, wl["pool_w"], wl["pool_scale"], n_seq, t_seq)
    c, ct = _fox_cum(lf, n_seq, t_seq)
    o_fox = _fox_prompt(qb, kb, vb, c, ct, n_seq, t_seq, 512)
    y = _outproj(x2d, o_rw, o_pool, o_fox, grw, gpool, gfox, gm, wl, 512)
    shift = zrw.reshape(n_seq, t_seq, RW_SHIFT)[:, -1]
    state = (s_new, shift, pool_buf,
             k.reshape(n_seq, t_seq, HEADS, HEAD_DIM), v.reshape(n_seq, t_seq, HEADS, HEAD_DIM),
             lf[:, :HEADS].reshape(n_seq, t_seq, HEADS))
    return y, state


def _sample_layer(x2d, wl, layer, n_seq, t_new, state_bl, st_shift, st_pool, caches, page_table):
    (gm, qb, k, v, _, _, grw, gpool, gfox, u, zrw, lf) = _inproj(
        x2d, wl["g_pre"], wl["w_cat"], wl["bf_pad"], 256, HEAD_DIM ** -0.5)
    tm = lambda a: jnp.transpose(a, (1, 0, 2))
    z3 = zrw.reshape(n_seq, t_new, RW_SHIFT)
    o_rw_tm, s_new_bl = _rwkv_sample(layer, tm(z3), st_shift, wl, state_bl)
    o_rw = tm(o_rw_tm).reshape(n_seq * t_new, BRANCH_W)
    s_new = jnp.transpose(s_new_bl, (3, 0, 1, 2))
    past = page_table.shape[1] * PAGE
    o_pool_tm, pool_buf_tm = _pool_sample(tm(st_pool), tm(u.reshape(n_seq, t_new, BRANCH_W)),
                                          wl["pool_w"], wl["pool_scale"], past)
    o_pool = tm(o_pool_tm).reshape(n_seq * t_new, BRANCH_W)
    cache_k, cache_v, cache_logf_t = caches
    tri = lambda a, w: a.reshape(n_seq, t_new, w)
    lf3 = lf[:, :HEADS].reshape(n_seq, t_new, HEADS)
    o_fox3 = _fox_sample(layer, page_table, tri(qb, BRANCH_W), tri(k, BRANCH_W), tri(v, BRANCH_W),
                         jnp.transpose(lf3, (0, 2, 1)), cache_k, cache_v, cache_logf_t)
    y = _outproj(x2d, o_rw, o_pool, o_fox3.reshape(n_seq * t_new, BRANCH_W), grw, gpool, gfox, gm,
                 wl, 256)
    state = (s_new, z3[:, -1], tm(pool_buf_tm),
             k.reshape(n_seq, t_new, HEADS, HEAD_DIM), v.reshape(n_seq, t_new, HEADS, HEAD_DIM), lf3)
    return y, state


def kernel(x_prompt, x_sample, state_rwkv, state_shift, state_pool, cache_k, cache_v, cache_logf, page_table, norm_pre, norm_post, w_in, b_forget, rw_mu, rw_w0, rw_w2, rw_a0, rw_a2, rw_kk, rw_ka, rw_rk, rw_lnw, rw_lnb, pool_w, pool_scale, proj_a, proj_b, proj_c, w_out):
    depth = w_in.shape[0]
    bp, sp, _ = x_prompt.shape
    bs, ts, _ = x_sample.shape
    params = dict(norm_pre=norm_pre, norm_post=norm_post, b_forget=b_forget, rw_mu=rw_mu, rw_w0=rw_w0,
                  rw_w2=rw_w2, rw_a0=rw_a0, rw_a2=rw_a2, rw_kk=rw_kk, rw_ka=rw_ka, rw_rk=rw_rk,
                  rw_lnw=rw_lnw, rw_lnb=rw_lnb, pool_w=pool_w, pool_scale=pool_scale, proj_a=proj_a,
                  proj_b=proj_b, proj_c=proj_c, w_out=w_out)
    w_cat = _reorder_w_in(w_in)
    n_pool = cache_k.shape[1]
    keys_minor = lambda c: jnp.transpose(c, (0, 1, 3, 4, 2)).reshape(depth, n_pool, BRANCH_W, PAGE)
    caches = (keys_minor(cache_k), keys_minor(cache_v), jnp.transpose(cache_logf, (0, 1, 3, 2)))
    state_bl = jnp.transpose(state_rwkv, (0, 2, 3, 4, 1))
    hp = x_prompt.reshape(bp * sp, D_MODEL)
    hs = x_sample.reshape(bs * ts, D_MODEL)
    st_p, st_s = [], []
    for l in range(depth):
        wl = _layer_weights(l, w_cat, params)
        hp, sp_state = _prompt_layer(hp, wl, bp, sp)
        hs, ss_state = _sample_layer(hs, wl, l, bs, ts, state_bl, state_shift[l], state_pool[l],
                                     caches, page_table)
        st_p.append(sp_state)
        st_s.append(ss_state)
    outs = [hp.reshape(bp, sp, D_MODEL), hs.reshape(bs, ts, D_MODEL)]
    for i in range(6):
        outs.append(jnp.stack([s[i] for s in st_p]))
        outs.append(jnp.stack([s[i] for s in st_s]))
    return tuple(outs)
```

```python
import functools

import jax
import jax.numpy as jnp
from jax import lax
from jax.experimental import pallas as pl
from jax.experimental.pallas import tpu as pltpu

F32 = jnp.float32
BF16 = jnp.bfloat16
HIGHEST = lax.Precision.HIGHEST

D_MODEL = 1024
HEADS = 8
HEAD_DIM = 64
BRANCH_W = HEADS * HEAD_DIM
LORA = 64
RW_SHIFT = 3 * BRANCH_W + 2 * LORA
POOL_WINDOWS = (2, 4, 8, 16)
POOL_GW = 128
POOL_BUF = 15
PAGE = 128
CHUNK = 64
CPI = 2
LANES = 128
EPS = 1e-6
LNX_EPS = 64e-5
LOG2E = 1.4426950408889634

C_MERGE = 0
C_QKV = 3 * D_MODEL
C_GRW = C_QKV + 3 * BRANCH_W
C_GPOOL = C_GRW + BRANCH_W
C_GFOX = C_GPOOL + BRANCH_W
C_UPOOL = C_GFOX + BRANCH_W
C_ZRW = C_UPOOL + BRANCH_W
C_F = C_ZRW + RW_SHIFT
W_COLS = C_F + LANES

VMEM_LIMIT = 56 * 1024 * 1024


def _cparams(sem):
    return pltpu.CompilerParams(dimension_semantics=sem, vmem_limit_bytes=VMEM_LIMIT)


def _sigmoid(x):
    return 1.0 / (1.0 + jnp.exp(-x))


def _log_sigmoid(x):
    return jnp.minimum(x, 0.0) - jnp.log1p(jnp.exp(-jnp.abs(x)))


def _dot(a, b):
    return jnp.dot(a.astype(BF16), b.astype(BF16), preferred_element_type=F32)


def _dot_nt(a, b):
    return lax.dot_general(a.astype(BF16), b.astype(BF16), (((1,), (1,)), ((), ())),
                           preferred_element_type=F32)


def _dot_tn(a, b):
    return lax.dot_general(a.astype(BF16), b.astype(BF16), (((0,), (0,)), ((), ())),
                           preferred_element_type=F32)


def _inproj_kernel(*refs, q_scale, keys_minor):
    x_ref, g_ref, w_ref, bf_ref = refs[:4]
    outs = refs[6:] if keys_minor else refs[4:]
    (gm_ref, q_ref, k_ref, v_ref, kb_ref, vb_ref, grw_ref, gpool_ref, gfox_ref, u_ref, zrw_ref,
     lf_ref) = outs
    x = x_ref[...]
    ms = jnp.mean(x * x, axis=-1, keepdims=True)
    xn = (x * lax.rsqrt(ms + EPS) * g_ref[...]).astype(BF16)

    def mm(c0, width):
        return jnp.dot(xn, w_ref[:, c0:c0 + width], preferred_element_type=F32)

    for i in range(3):
        gm_ref[:, i * D_MODEL:(i + 1) * D_MODEL] = _sigmoid(mm(C_MERGE + i * D_MODEL, D_MODEL))
    q_ref[...] = (mm(C_QKV, BRANCH_W) * q_scale).astype(BF16)
    k = mm(C_QKV + BRANCH_W, BRANCH_W)
    k_ref[...] = k.T if keys_minor else k
    kb_ref[...] = k.astype(BF16)
    v = mm(C_QKV + 2 * BRANCH_W, BRANCH_W)
    v_ref[...] = v.T if keys_minor else v
    vb_ref[...] = v.astype(BF16)
    for ref, c0 in ((grw_ref, C_GRW), (gpool_ref, C_GPOOL), (gfox_ref, C_GFOX)):
        g = mm(c0, BRANCH_W)
        ref[...] = g * _sigmoid(g)
    u_ref[...] = mm(C_UPOOL, BRANCH_W)
    zrw_ref[:, 0:3 * BRANCH_W] = mm(C_ZRW, 3 * BRANCH_W)
    zrw_ref[:, 3 * BRANCH_W:RW_SHIFT] = mm(C_ZRW + 3 * BRANCH_W, 2 * LORA)
    f = mm(C_F, LANES) + bf_ref[...]
    head_lane = lax.broadcasted_iota(jnp.int32, f.shape, 1) < HEADS
    lf_ref[...] = jnp.where(head_lane, _log_sigmoid(f), 0.0)


def _inproj(x2d, g_pre, w_cat, bf_pad, tm, q_scale, kv_bufs=None, layer=0):
    n = x2d.shape[0]
    row = lambda i: (i, 0)
    const = lambda i: (0, 0)
    widths = [(3 * D_MODEL, F32), (BRANCH_W, BF16), (BRANCH_W, F32), (BRANCH_W, F32),
              (BRANCH_W, BF16), (BRANCH_W, BF16), (BRANCH_W, F32), (BRANCH_W, F32),
              (BRANCH_W, F32), (BRANCH_W, F32), (RW_SHIFT, F32), (LANES, F32)]
    in_specs = [pl.BlockSpec((tm, D_MODEL), row),
                pl.BlockSpec((1, D_MODEL), const),
                pl.BlockSpec((D_MODEL, W_COLS), const, pipeline_mode=pl.Buffered(1)),
                pl.BlockSpec((1, LANES), const)]
    out_specs = [pl.BlockSpec((tm, w), row) for w, _ in widths]
    out_shape = [jax.ShapeDtypeStruct((n, w), dt) for w, dt in widths]
    args = [x2d, g_pre, w_cat, bf_pad]
    aliases = {}
    if kv_bufs is not None:
        tiles = kv_bufs[0].shape[3] // tm
        kv_spec = pl.BlockSpec((None, None, BRANCH_W, tm), lambda i: (layer, i // tiles, 0, i % tiles))
        for j, buf in enumerate(kv_bufs):
            in_specs.append(pl.BlockSpec(memory_space=pl.ANY))
            args.append(buf)
            out_specs[2 + j] = kv_spec
            out_shape[2 + j] = jax.ShapeDtypeStruct(buf.shape, buf.dtype)
            aliases[4 + j] = 2 + j
    return pl.pallas_call(
        functools.partial(_inproj_kernel, q_scale=q_scale, keys_minor=kv_bufs is not None),
        grid=(n // tm,),
        in_specs=in_specs,
        out_specs=out_specs,
        out_shape=out_shape,
        input_output_aliases=aliases,
        compiler_params=_cparams(("parallel",)),
        name="inproj",
    )(*args)


def _seg_sum(x, lo_mask):
    s_lo = jnp.sum(jnp.where(lo_mask, x, 0.0), axis=-1, keepdims=True)
    s_hi = jnp.sum(jnp.where(lo_mask, 0.0, x), axis=-1, keepdims=True)
    return jnp.where(lo_mask, s_lo, s_hi)


def _seg_sum_wide(x):
    tt = x.shape[0]
    lo_mask = lax.broadcasted_iota(jnp.int32, (tt, LANES), 1) < HEAD_DIM
    parts = [_seg_sum(x[:, p * LANES:(p + 1) * LANES], lo_mask) for p in range(x.shape[1] // LANES)]
    return jnp.concatenate(parts, axis=-1)


def _rwkv_pre_kernel(z_ref, zp_ref, zl_ref, mu_ref, w0_ref, w2_ref, a0_ref, a2_ref, kk_ref, ka_ref,
                     rk_ref, at_ref, rt_ref, bt_ref, kt_ref, vb_ref, bonus_ref, pc_ref,
                     *, tt, t_seq):
    i = pl.program_id(0)
    z = z_ref[...]
    starts_seq = (i * tt) % t_seq == 0
    prev_row = jnp.where(starts_seq, zl_ref[...], zp_ref[7:8, :])
    row = lax.broadcasted_iota(jnp.int32, (tt, 1), 0)
    z_prev = jnp.where(row == 0, prev_row, pltpu.roll(z, 1, axis=0))
    zs = z + (z_prev - z) * mu_ref[...]
    w3 = BRANCH_W
    r = zs[:, 0:w3]
    k = zs[:, w3:2 * w3]
    v = zs[:, 2 * w3:3 * w3]
    lora = zs[:, 3 * w3:RW_SHIFT]
    wx = w0_ref[...] + jnp.dot(jnp.tanh(lora).astype(BF16), w2_ref[...], preferred_element_type=F32)
    logd = -jnp.exp(_log_sigmoid(wx) - 0.5)
    a = _sigmoid(a0_ref[...] + jnp.dot(lora.astype(BF16), a2_ref[...], preferred_element_type=F32))
    kk = k * kk_ref[...]
    k2 = k * (1.0 + (a - 1.0) * ka_ref[...])
    kk = kk / jnp.maximum(jnp.sqrt(_seg_sum_wide(kk * kk)), 1e-12)
    bonus_ref[...] = _seg_sum_wide(r * k2 * rk_ref[...]) * v
    ri = lax.broadcasted_iota(jnp.int32, (CHUNK, CHUNK), 0)
    ci = lax.broadcasted_iota(jnp.int32, (CHUNK, CHUNK), 1)
    tri = jnp.where(ci <= ri, 1.0, 0.0).astype(F32)
    cl = jnp.concatenate(
        [jnp.dot(tri, logd[c * CHUNK:(c + 1) * CHUNK, :], preferred_element_type=F32, precision=HIGHEST)
         for c in range(tt // CHUNK)], axis=0)
    e_neg = jnp.exp(-cl)
    at_ref[...] = (-kk * jnp.exp(cl - logd)).astype(BF16)
    rt_ref[...] = (r * jnp.exp(cl)).astype(BF16)
    bt_ref[...] = (kk * a * e_neg).astype(BF16)
    kt_ref[...] = (k2 * e_neg).astype(BF16)
    vb_ref[...] = v.astype(BF16)
    for c in range(tt // CHUNK):
        last = cl[(c + 1) * CHUNK - 1:(c + 1) * CHUNK, :]
        pc_ref[c * 8:(c + 1) * 8, :] = jnp.broadcast_to(jnp.exp(last), (8, BRANCH_W))


def _rwkv_pre(z, z_last, wl, tt, t_seq):
    n = z.shape[0]
    row = lambda i: (i, 0)
    const = lambda i: (0, 0)
    vec = lambda w: pl.BlockSpec((1, w), const)
    nb8 = tt // 8
    kern = functools.partial(_rwkv_pre_kernel, tt=tt, t_seq=t_seq)
    wide = lambda dt: jax.ShapeDtypeStruct((n, BRANCH_W), dt)
    return pl.pallas_call(
        kern,
        grid=(n // tt,),
        in_specs=[pl.BlockSpec((tt, RW_SHIFT), row),
                  pl.BlockSpec((8, RW_SHIFT), lambda i: (jnp.maximum(i * nb8 - 1, 0), 0)),
                  pl.BlockSpec((None, 1, RW_SHIFT), lambda i: ((i * tt) // t_seq, 0, 0)),
                  vec(RW_SHIFT), vec(BRANCH_W),
                  pl.BlockSpec((LANES, BRANCH_W), const),
                  vec(BRANCH_W),
                  pl.BlockSpec((LANES, BRANCH_W), const),
                  vec(BRANCH_W), vec(BRANCH_W), vec(BRANCH_W)],
        out_specs=[pl.BlockSpec((tt, BRANCH_W), row)] * 6
                  + [pl.BlockSpec((tt // CHUNK * 8, BRANCH_W), row)],
        out_shape=[wide(BF16), wide(BF16), wide(BF16), wide(BF16), wide(BF16), wide(F32),
                   jax.ShapeDtypeStruct((n // CHUNK * 8, BRANCH_W), F32)],
        compiler_params=_cparams(("parallel",)),
        name="rwkv_pre",
    )(z, z, z_last, wl["mu"], wl["w0"], wl["w2p"], wl["a0"], wl["a2p"], wl["kk"], wl["ka"], wl["rk"])


def _rwkv_chunk_kernel(at_ref, rt_ref, bt_ref, kt_ref, vb_ref, bonus_ref, pc_ref, lnw_ref,
                       lnb_ref, o_ref, sout_ref, s_scr, *, cpb):
    g = pl.program_id(1)
    c2 = 2 * CHUNK
    pairs = HEADS // 2

    @pl.when(g == 0)
    def _():
        s_scr[...] = jnp.zeros(s_scr.shape, F32)

    lane = lax.broadcasted_iota(jnp.int32, (CHUNK, LANES), 1)
    lo = lane < HEAD_DIM
    ri = lax.broadcasted_iota(jnp.int32, (c2, c2), 0)
    ci = lax.broadcasted_iota(jnp.int32, (c2, c2), 1)
    same = (ri // CHUNK) == (ci // CHUNK)
    strict = same & (ci < ri)
    incl = same & (ci <= ri)
    eye = jnp.where(ri == ci, 1.0, 0.0).astype(F32)

    def stack2(x):
        x = x.astype(F32)
        return jnp.concatenate([jnp.where(lo, x, 0.0), jnp.where(lo, 0.0, x)], axis=0)

    def body(i, carry):
        rows = [pl.ds(pl.multiple_of((i * CPI + cc) * CHUNK, CHUNK), CHUNK) for cc in range(CPI)]
        pc_row = [pl.ds(pl.multiple_of((i * CPI + cc) * 8, 8), 1) for cc in range(CPI)]
        lanes = [slice(p * LANES, (p + 1) * LANES) for p in range(pairs)]
        units = [(rows[cc], lanes[p]) for cc in range(CPI) for p in range(pairs)]
        each = lambda f, *xs: [f(*args) for args in zip(*xs)]
        a_s = [stack2(at_ref[rw, ln]) for rw, ln in units]
        r_s = [stack2(rt_ref[rw, ln]) for rw, ln in units]
        b_s = [stack2(bt_ref[rw, ln]) for rw, ln in units]
        k_s = [stack2(kt_ref[rw, ln]) for rw, ln in units]
        vp = [vb_ref[rw, ln].astype(F32) for rw, ln in units]
        v_s = [jnp.concatenate([x[:, :HEAD_DIM], x[:, HEAD_DIM:]], axis=0) for x in vp]
        pc = [pc_ref[pc_row[cc], lanes[p]] for cc in range(CPI) for p in range(pairs)]
        nmat = each(lambda a, b: jnp.where(strict, _dot_nt(a, b), 0.0), a_s, b_s)
        lak = each(lambda a, k: jnp.where(strict, _dot_nt(a, k), 0.0), a_s, k_s)
        mrb = each(lambda r, b: jnp.where(incl, _dot_nt(r, b), 0.0), r_s, b_s)
        mrk = each(lambda r, k: jnp.where(incl, _dot_nt(r, k), 0.0), r_s, k_s)
        lakv = each(_dot, lak, v_s)
        mrkv = each(_dot, mrk, v_s)
        vtk = each(_dot_tn, v_s, k_s)
        tinv = [eye + n for n in nmat]
        pw = nmat
        for _ in range(5):
            pw = each(_dot, pw, pw)
            tinv = each(lambda t, q: t + _dot(t, q), tinv, pw)
        a_hat = each(_dot, tinv, a_s)
        u_hat = each(_dot, tinv, lakv)
        r_hat = each(lambda r, m, a: r + _dot(m, a), r_s, mrb, a_hat)
        y_hat = each(lambda m, u, y: _dot(m, u) + y, mrb, u_hat, mrkv)
        gmat = each(lambda a, b, x: (eye + _dot_tn(a, b)) * x, a_hat, b_s, pc)
        hmat = each(lambda u, b, y, x: (_dot_tn(u, b) + y) * x, u_hat, b_s, vtk, pc)
        s = [s_scr[p] for p in range(pairs)]
        for cc in range(CPI):
            sel = slice(cc * pairs, (cc + 1) * pairs)
            y_s = each(lambda r, st, y: _dot_nt(r, st) + y, r_hat[sel], s, y_hat[sel])
            s = each(lambda st, gm, hm: _dot(st, gm) + hm, s, gmat[sel], hmat[sel])
            for p in range(pairs):
                y = jnp.concatenate([y_s[p][:CHUNK], y_s[p][CHUNK:]], axis=1)
                mean = _seg_sum(y, lo) * (1.0 / HEAD_DIM)
                yc = y - mean
                var = _seg_sum(yc * yc, lo) * (1.0 / HEAD_DIM)
                ln = lanes[p]
                o_ref[rows[cc], ln] = (yc * lax.rsqrt(var + LNX_EPS) * lnw_ref[:, ln] + lnb_ref[:, ln]
                                       + bonus_ref[rows[cc], ln])
        for p in range(pairs):
            s_scr[p] = s[p]
        return carry

    lax.fori_loop(0, cpb // CPI, body, 0)

    @pl.when(g == pl.num_programs(1) - 1)
    def _():
        for p in range(pairs):
            s = s_scr[p]
            sout_ref[2 * p] = s[:, :HEAD_DIM]
            sout_ref[2 * p + 1] = s[:, HEAD_DIM:]


def _rwkv_chunk(pre, lnw, lnb, n_seq, t_seq, cpb):
    at, rt, bt, kt, vb, bonus, pc = pre
    n = at.shape[0]
    ng = t_seq // (cpb * CHUNK)
    rows = cpb * CHUNK
    tile = lambda b, g: (b * ng + g, 0)
    const = lambda b, g: (0, 0)
    state = pl.BlockSpec((None, HEADS, HEAD_DIM, HEAD_DIM), lambda b, g: (b, 0, 0, 0))
    kern = functools.partial(_rwkv_chunk_kernel, cpb=cpb)
    return pl.pallas_call(
        kern,
        grid=(n_seq, ng),
        in_specs=[pl.BlockSpec((rows, BRANCH_W), tile)] * 6
                 + [pl.BlockSpec((cpb * 8, BRANCH_W), tile),
                    pl.BlockSpec((1, BRANCH_W), const),
                    pl.BlockSpec((1, BRANCH_W), const)],
        out_specs=[pl.BlockSpec((rows, BRANCH_W), tile), state],
        out_shape=[jax.ShapeDtypeStruct((n, BRANCH_W), F32),
                   jax.ShapeDtypeStruct((n_seq, HEADS, HEAD_DIM, HEAD_DIM), F32)],
        scratch_shapes=[pltpu.VMEM((HEADS // 2, HEAD_DIM, LANES), F32)],
        compiler_params=_cparams(("parallel", "arbitrary")),
        name="rwkv_chunk",
    )(at, rt, bt, kt, vb, bonus, pc, lnw, lnb)


def _rwkv_sample_kernel(zr_ref, zk_ref, zv_ref, zl_ref, lr_ref, lk_ref, lv_ref, ll_ref,
                        mur_ref, muk_ref, muv_ref, mul_ref, w0_ref, w2_ref, a0_ref, a2_ref, kk_ref,
                        ka_ref, rk_ref, lnw_ref, lnb_ref, s_ref, o_ref, sout_ref, tr_ref, yt_ref,
                        *, t_new):
    n_b = zr_ref.shape[1]
    lo = lax.broadcasted_iota(jnp.int32, (n_b, LANES), 1) < HEAD_DIM

    def shifted(cur_ref, last_ref, mu_ref, t):
        cur = cur_ref[t]
        prev = last_ref[...] if t == 0 else cur_ref[t - 1]
        return cur + (prev - cur) * mu_ref[...]

    for t in range(t_new):
        r = shifted(zr_ref, lr_ref, mur_ref, t)
        k = shifted(zk_ref, lk_ref, muk_ref, t)
        v = shifted(zv_ref, lv_ref, muv_ref, t)
        lora = shifted(zl_ref, ll_ref, mul_ref, t)
        wx = w0_ref[...] + jnp.dot(jnp.tanh(lora).astype(BF16), w2_ref[...], preferred_element_type=F32)
        logd = -jnp.exp(_log_sigmoid(wx) - 0.5)
        a = _sigmoid(a0_ref[...] + jnp.dot(lora.astype(BF16), a2_ref[...], preferred_element_type=F32))
        kk = k * kk_ref[...]
        k2 = k * (1.0 + (a - 1.0) * ka_ref[...])
        kk = kk / jnp.maximum(jnp.sqrt(_seg_sum(kk * kk, lo)), 1e-12)
        o_ref[t] = _seg_sum(r * k2 * rk_ref[...], lo) * v
        for i, x in enumerate((-kk, jnp.exp(logd), kk * a, k2, r, v)):
            tr_ref[t, i] = x.T

    for hh in range(2):
        h0 = hh * HEAD_DIM

        def row_step(vi, carry):
            s_v = s_ref[hh, vi]
            for t in range(t_new):
                nkk = tr_ref[t, 0, h0:h0 + HEAD_DIM, :]
                dec = tr_ref[t, 1, h0:h0 + HEAD_DIM, :]
                kka = tr_ref[t, 2, h0:h0 + HEAD_DIM, :]
                k2 = tr_ref[t, 3, h0:h0 + HEAD_DIM, :]
                r = tr_ref[t, 4, h0:h0 + HEAD_DIM, :]
                vv = tr_ref[t, 5, pl.ds(h0 + vi, 1), :]
                sa = jnp.sum(s_v * nkk, axis=0, keepdims=True)
                s_v = s_v * dec + sa * kka + vv * k2
                yt_ref[t, pl.ds(h0 + vi, 1), :] = jnp.sum(s_v * r, axis=0, keepdims=True)
            sout_ref[hh, vi] = s_v
            return carry

        lax.fori_loop(0, HEAD_DIM, row_step, 0, unroll=4)

    for t in range(t_new):
        y = yt_ref[t].T
        mean = _seg_sum(y, lo) * (1.0 / HEAD_DIM)
        yc = y - mean
        var = _seg_sum(yc * yc, lo) * (1.0 / HEAD_DIM)
        o_ref[t] = o_ref[t] + yc * lax.rsqrt(var + LNX_EPS) * lnw_ref[...] + lnb_ref[...]


def _rwkv_sample(layer, zt, z_last, wl, state_bl):
    t_new, n_b, _ = zt.shape
    pairs = HEADS // 2
    lora_tile = 3 * pairs
    seg = lambda s: pl.BlockSpec((t_new, n_b, LANES), lambda p, s=s: (0, 0, s * pairs + p))
    seg_last = lambda s: pl.BlockSpec((n_b, LANES), lambda p, s=s: (0, s * pairs + p))
    seg_mu = lambda s: pl.BlockSpec((1, LANES), lambda p, s=s: (0, s * pairs + p))
    vec = pl.BlockSpec((1, LANES), lambda p: (0, p))
    lora_w = pl.BlockSpec((LANES, LANES), lambda p: (0, p))
    kern = functools.partial(_rwkv_sample_kernel, t_new=t_new)
    return pl.pallas_call(
        kern,
        grid=(pairs,),
        in_specs=[seg(0), seg(1), seg(2),
                  pl.BlockSpec((t_new, n_b, LANES), lambda p: (0, 0, lora_tile)),
                  seg_last(0), seg_last(1), seg_last(2),
                  pl.BlockSpec((n_b, LANES), lambda p: (0, lora_tile)),
                  seg_mu(0), seg_mu(1), seg_mu(2),
                  pl.BlockSpec((1, LANES), lambda p: (0, lora_tile)),
                  vec, lora_w, vec, lora_w, vec, vec, vec, vec, vec,
                  pl.BlockSpec((None, 2, HEAD_DIM, HEAD_DIM, n_b), lambda p: (layer, p, 0, 0, 0))],
        out_specs=[pl.BlockSpec((t_new, n_b, LANES), lambda p: (0, 0, p)),
                   pl.BlockSpec((2, HEAD_DIM, HEAD_DIM, n_b), lambda p: (p, 0, 0, 0))],
        out_shape=[jax.ShapeDtypeStruct((t_new, n_b, BRANCH_W), F32),
                   jax.ShapeDtypeStruct((HEADS, HEAD_DIM, HEAD_DIM, n_b), F32)],
        scratch_shapes=[pltpu.VMEM((t_new, 6, LANES, n_b), F32), pltpu.VMEM((t_new, LANES, n_b), F32)],
        compiler_params=_cparams(("parallel",)),
        name="rwkv_sample",
    )(zt, zt, zt, zt, z_last, z_last, z_last, z_last, wl["mu"], wl["mu"], wl["mu"], wl["mu"],
      wl["w0"], wl["w2p"], wl["a0"], wl["a2p"], wl["kk"], wl["ka"], wl["rk"], wl["lnw"], wl["lnb"],
      state_bl)


def _pool_prompt_kernel(u_ref, pw_ref, ps_ref, o_ref, buf_ref, ext_ref, *, t_seq, tile):
    pad = POOL_BUF + 1
    ext_ref[0:pad, :] = jnp.zeros((pad, BRANCH_W), F32)
    ext_ref[pad:pad + t_seq, :] = u_ref[...]
    buf_ref[...] = u_ref[t_seq - POOL_BUF:t_seq, :]
    for t0 in range(0, t_seq, tile):
        pos = t0 + lax.broadcasted_iota(jnp.int32, (tile, 1), 0)
        for gi, w in enumerate(POOL_WINDOWS):
            sl = slice(gi * POOL_GW, (gi + 1) * POOL_GW)
            wsum = ext_ref[pad + t0:pad + t0 + tile, sl]
            for j in range(1, w):
                wsum = wsum + ext_ref[pad + t0 - j:pad + t0 - j + tile, sl]
            cnt = jnp.minimum(pos + 1, w).astype(F32)
            d = wsum / cnt - ext_ref[pad + t0:pad + t0 + tile, sl]
            o_ref[t0:t0 + tile, sl] = _dot(d, pw_ref[gi]) * ps_ref[:, sl]


def _pool_prompt(u, pool_w, pool_scale, n_seq, t_seq):
    kern = functools.partial(_pool_prompt_kernel, t_seq=t_seq, tile=256)
    return pl.pallas_call(
        kern,
        grid=(n_seq,),
        in_specs=[pl.BlockSpec((t_seq, BRANCH_W), lambda b: (b, 0)),
                  pl.BlockSpec((len(POOL_WINDOWS), POOL_GW, POOL_GW), lambda b: (0, 0, 0)),
                  pl.BlockSpec((1, BRANCH_W), lambda b: (0, 0))],
        out_specs=[pl.BlockSpec((t_seq, BRANCH_W), lambda b: (b, 0)),
                   pl.BlockSpec((None, POOL_BUF, BRANCH_W), lambda b: (b, 0, 0))],
        out_shape=[jax.ShapeDtypeStruct((n_seq * t_seq, BRANCH_W), F32),
                   jax.ShapeDtypeStruct((n_seq, POOL_BUF, BRANCH_W), F32)],
        scratch_shapes=[pltpu.VMEM((POOL_BUF + 1 + t_seq, BRANCH_W), F32)],
        compiler_params=_cparams(("parallel",)),
        name="pool_prompt",
    )(u, pool_w, pool_scale)


def _pool_sample_kernel(pre_ref, u_ref, pw_ref, ps_ref, o_ref, buf_ref, *, t_new, pos0):
    def ext_row(i, sl):
        if i < POOL_BUF:
            return pre_ref[i, :, sl]
        return u_ref[i - POOL_BUF, :, sl]

    full = slice(0, BRANCH_W)
    for i in range(POOL_BUF):
        buf_ref[i] = ext_row(i + t_new, full)
    for t in range(t_new):
        for gi, w in enumerate(POOL_WINDOWS):
            sl = slice(gi * POOL_GW, (gi + 1) * POOL_GW)
            cur = ext_row(POOL_BUF + t, sl)
            wsum = cur
            for j in range(1, w):
                wsum = wsum + ext_row(POOL_BUF + t - j, sl)
            cnt = float(min(pos0 + t + 1, w))
            d = wsum / cnt - cur
            o_ref[t, :, sl] = _dot(d, pw_ref[gi]) * ps_ref[:, sl]


def _pool_sample(prefix, u3, pool_w, pool_scale, pos0):
    t_new = u3.shape[0]
    kern = functools.partial(_pool_sample_kernel, t_new=t_new, pos0=pos0)
    full3 = lambda shape: pl.BlockSpec(shape, lambda i: (0, 0, 0))
    return pl.pallas_call(
        kern,
        grid=(1,),
        in_specs=[full3(prefix.shape), full3(u3.shape), full3(pool_w.shape),
                  pl.BlockSpec((1, BRANCH_W), lambda i: (0, 0))],
        out_specs=[full3(u3.shape), full3(prefix.shape)],
        out_shape=[jax.ShapeDtypeStruct(u3.shape, F32), jax.ShapeDtypeStruct(prefix.shape, F32)],
        compiler_params=_cparams(("arbitrary",)),
        name="pool_sample",
    )(prefix, u3, pool_w, pool_scale)


def _fox_cum_kernel(lf_ref, c_ref, ct_ref, *, t_seq, tile):
    ri = lax.broadcasted_iota(jnp.int32, (tile, tile), 0)
    ci = lax.broadcasted_iota(jnp.int32, (tile, tile), 1)
    tri = jnp.where(ci <= ri, 1.0, 0.0).astype(F32)
    e8 = jnp.where(lax.broadcasted_iota(jnp.int32, (HEADS, LANES), 0)
                   == lax.broadcasted_iota(jnp.int32, (HEADS, LANES), 1), 1.0, 0.0).astype(F32)
    carry = jnp.zeros((1, LANES), F32)
    for t0 in range(0, t_seq, tile):
        c = jnp.dot(tri, lf_ref[t0:t0 + tile, :], preferred_element_type=F32, precision=HIGHEST) + carry
        c_ref[t0:t0 + tile, :] = c
        ct_ref[:, t0:t0 + tile] = lax.dot_general(e8, c, (((1,), (1,)), ((), ())),
                                                  preferred_element_type=F32, precision=HIGHEST)
        carry = c[tile - 1:tile, :]


def _fox_cum(lf, n_seq, t_seq):
    kern = functools.partial(_fox_cum_kernel, t_seq=t_seq, tile=256)
    return pl.pallas_call(
        kern,
        grid=(n_seq,),
        in_specs=[pl.BlockSpec((t_seq, LANES), lambda b: (b, 0))],
        out_specs=[pl.BlockSpec((t_seq, LANES), lambda b: (b, 0)),
                   pl.BlockSpec((None, HEADS, t_seq), lambda b: (b, 0, 0))],
        out_shape=[jax.ShapeDtypeStruct((n_seq * t_seq, LANES), F32),
                   jax.ShapeDtypeStruct((n_seq, HEADS, t_seq), F32)],
        compiler_params=_cparams(("parallel",)),
        name="fox_cum",
    )(lf)


def _fox_prompt_kernel(q_ref, k_ref, v_ref, c_ref, ct_ref, o_ref, *, tq):
    p = pl.program_id(1)
    qi = pl.program_id(2)
    q = q_ref[...].astype(F32)
    lane = lax.broadcasted_iota(jnp.int32, (tq, LANES), 1)
    lo = lane < HEAD_DIM
    ri = lax.broadcasted_iota(jnp.int32, (tq, tq), 0)
    ci = lax.broadcasted_iota(jnp.int32, (tq, tq), 1)
    causal = ci <= ri
    c_all = c_ref[...]
    qh = [jnp.where(lo, q, 0.0).astype(BF16), jnp.where(lo, 0.0, q).astype(BF16)]
    cq = [jnp.sum(jnp.where(lane == 2 * p + hh, c_all, 0.0), axis=-1, keepdims=True) * LOG2E
          for hh in range(2)]

    def block(j, carry, masked):
        cols = pl.ds(pl.multiple_of(j * tq, tq), tq)
        k = k_ref[cols, :]
        v = v_ref[cols, :]
        heads = range(2)
        ck = [ct_ref[pl.ds(2 * p + hh, 1), cols] * LOG2E for hh in heads]
        s = [lax.dot_general(qh[hh], k, (((1,), (1,)), ((), ())), preferred_element_type=F32) - ck[hh]
             for hh in heads]
        if masked:
            s = [jnp.where(causal, x, -jnp.inf) for x in s]
        m_new = [jnp.maximum(carry[hh][0], jnp.max(s[hh], axis=-1, keepdims=True) + cq[hh]) for hh in heads]
        pr = [jnp.exp2(s[hh] - (m_new[hh] - cq[hh])) for hh in heads]
        alpha = [jnp.exp2(carry[hh][0] - m_new[hh]) for hh in heads]
        l = [alpha[hh] * carry[hh][1] + jnp.sum(pr[hh], axis=-1, keepdims=True) for hh in heads]
        acc = [alpha[hh] * carry[hh][2] + jnp.dot(pr[hh].astype(BF16), v, preferred_element_type=F32)
               for hh in heads]
        return tuple((m_new[hh], l[hh], acc[hh]) for hh in heads)

    one = (jnp.full((tq, 1), -jnp.inf, F32), jnp.zeros((tq, 1), F32), jnp.zeros((tq, LANES), F32))
    carry = lax.fori_loop(0, qi, lambda j, c: block(j, c, False), (one, one))
    (_, l0, acc0), (_, l1, acc1) = block(qi, carry, True)
    o_ref[...] = jnp.where(lo, acc0 / l0, acc1 / l1)


def _fox_prompt(qb, kb, vb, c, ct, n_seq, t_seq, tq):
    nq = t_seq // tq
    pairs = HEADS // 2
    kern = functools.partial(_fox_prompt_kernel, tq=tq)
    return pl.pallas_call(
        kern,
        grid=(n_seq, pairs, nq),
        in_specs=[pl.BlockSpec((tq, LANES), lambda b, p, i: (b * nq + i, p)),
                  pl.BlockSpec((t_seq, LANES), lambda b, p, i: (b, p)),
                  pl.BlockSpec((t_seq, LANES), lambda b, p, i: (b, p)),
                  pl.BlockSpec((tq, LANES), lambda b, p, i: (b * nq + i, 0)),
                  pl.BlockSpec((None, HEADS, t_seq), lambda b, p, i: (b, 0, 0))],
        out_specs=pl.BlockSpec((tq, LANES), lambda b, p, i: (b * nq + i, p)),
        out_shape=jax.ShapeDtypeStruct((n_seq * t_seq, BRANCH_W), F32),
        compiler_params=_cparams(("parallel", "parallel", "arbitrary")),
        name="fox_prompt",
    )(qb, kb, vb, c, ct)


def _fox_sample_kernel(pt_ref, q_ref, kn_ref, vn_ref, lfn_ref, *refs, n_pages, t_new):
    del pt_ref
    k_pages = refs[0:n_pages]
    v_pages = refs[n_pages:2 * n_pages]
    lf_pages = refs[2 * n_pages:3 * n_pages]
    o_ref = refs[3 * n_pages]
    past = n_pages * PAGE
    rows = t_new * HEADS
    nt = (((1,), (1,)), ((), ()))
    lf_t = jnp.concatenate([lf_pages[i][...] for i in range(n_pages)], axis=1)
    pos = lax.broadcasted_iota(jnp.int32, (HEADS, past), 1)
    x = lf_t
    sh = 1
    while sh < past:
        x = x + jnp.where(pos + sh < past, pltpu.roll(x, past - sh, axis=1), 0.0)
        sh *= 2
    suffix = x - lf_t
    lfn = lfn_ref[...]
    cn = [lfn[:, 0:1]]
    for t in range(1, t_new):
        cn.append(cn[-1] + lfn[:, t:t + 1])
    q = q_ref[...].astype(F32)
    head_of_lane = lax.broadcasted_iota(jnp.int32, (HEADS, BRANCH_W), 1) // HEAD_DIM
    head_mask = head_of_lane == lax.broadcasted_iota(jnp.int32, (HEADS, BRANCH_W), 0)
    qbd = jnp.concatenate(
        [jnp.where(head_mask, jnp.broadcast_to(q[t:t + 1, :], (HEADS, BRANCH_W)), 0.0)
         for t in range(t_new)], axis=0)
    qbd_bf = qbd.astype(BF16)
    s_past = jnp.concatenate(
        [jnp.dot(qbd_bf, k_pages[i][...].astype(BF16), preferred_element_type=F32)
         for i in range(n_pages)], axis=1)
    s_past = s_past + jnp.concatenate([suffix + cn[t] for t in range(t_new)], axis=0)
    row_t = lax.broadcasted_iota(jnp.int32, (rows, 1), 0) // HEADS
    kn = kn_ref[...]
    vn = vn_ref[...]
    s_new = []
    for j in range(t_new):
        dots = jnp.sum(qbd * kn[j:j + 1, :], axis=-1, keepdims=True)
        bias = jnp.concatenate([cn[t] - cn[j] for t in range(t_new)], axis=0)
        s_new.append(jnp.where(row_t >= j, dots + bias, -jnp.inf))
    m = jnp.max(s_past, axis=-1, keepdims=True)
    for j in range(t_new):
        m = jnp.maximum(m, s_new[j])
    p_past = jnp.exp(s_past - m)
    l = jnp.sum(p_past, axis=-1, keepdims=True)
    o = jnp.zeros((rows, BRANCH_W), F32)
    for j in range(t_new):
        p_j = jnp.exp(s_new[j] - m)
        l = l + p_j
        o = o + p_j * vn[j:j + 1, :]
    for i in range(n_pages):
        o = o + lax.dot_general(p_past[:, i * PAGE:(i + 1) * PAGE].astype(BF16),
                                v_pages[i][...].astype(BF16), nt, preferred_element_type=F32)
    o = o / l
    for t in range(t_new):
        blk = jnp.where(head_mask, o[t * HEADS:(t + 1) * HEADS, :], 0.0)
        o_ref[t:t + 1, :] = jnp.sum(blk, axis=0, keepdims=True)


def _fox_sample(layer, page_table, q3, k3, v3, lf3t, cache_k, cache_v, cache_logf_t):
    n_seq, t_new, _ = q3.shape
    n_pages = page_table.shape[1]
    kern = functools.partial(_fox_sample_kernel, n_pages=n_pages, t_new=t_new)
    new_spec = lambda w: pl.BlockSpec((None, t_new, w), lambda b, pt: (b, 0, 0))

    def page_spec(i, rows, w):
        return pl.BlockSpec((None, None, rows, w), lambda b, pt: (layer, pt[b, i], 0, 0))

    grid_spec = pltpu.PrefetchScalarGridSpec(
        num_scalar_prefetch=1,
        grid=(n_seq,),
        in_specs=[new_spec(BRANCH_W), new_spec(BRANCH_W), new_spec(BRANCH_W),
                  pl.BlockSpec((None, HEADS, t_new), lambda b, pt: (b, 0, 0))]
                 + [page_spec(i, BRANCH_W, PAGE) for i in range(n_pages)]
                 + [page_spec(i, BRANCH_W, PAGE) for i in range(n_pages)]
                 + [page_spec(i, HEADS, PAGE) for i in range(n_pages)],
        out_specs=pl.BlockSpec((None, t_new, BRANCH_W), lambda b, pt: (b, 0, 0)),
    )
    return pl.pallas_call(
        kern,
        grid_spec=grid_spec,
        out_shape=jax.ShapeDtypeStruct((n_seq, t_new, BRANCH_W), F32),
        compiler_params=_cparams(("parallel",)),
        name="fox_sample",
    )(page_table, q3, k3, v3, lf3t, *([cache_k] * n_pages), *([cache_v] * n_pages),
      *([cache_logf_t] * n_pages))


def _outproj_kernel(x_ref, oa_ref, ob_ref, oc_ref, ga_ref, gb_ref, gc_ref, gm_ref, pa_ref, pb_ref,
                    pc_ref, wo_ref, gp_ref, y_ref):
    ya = _dot(oa_ref[...] * ga_ref[...], pa_ref[...])
    yb = _dot(ob_ref[...] * gb_ref[...], pb_ref[...])
    yc = _dot(oc_ref[...] * gc_ref[...], pc_ref[...])
    m = (gm_ref[:, 0:D_MODEL] * ya + gm_ref[:, D_MODEL:2 * D_MODEL] * yb
         + gm_ref[:, 2 * D_MODEL:3 * D_MODEL] * yc)
    y = _dot(m, wo_ref[...])
    ms = jnp.mean(y * y, axis=-1, keepdims=True)
    y_ref[...] = x_ref[...] + y * lax.rsqrt(ms + EPS) * gp_ref[...]


def _outproj(x2d, oa, ob, oc, ga, gb, gc, gm, wl, tm):
    n = x2d.shape[0]
    row = lambda i: (i, 0)
    const = lambda i: (0, 0)
    wide = pl.BlockSpec((tm, BRANCH_W), row)
    proj = pl.BlockSpec((BRANCH_W, D_MODEL), const)
    return pl.pallas_call(
        _outproj_kernel,
        grid=(n // tm,),
        in_specs=[pl.BlockSpec((tm, D_MODEL), row), wide, wide, wide, wide, wide, wide,
                  pl.BlockSpec((tm, 3 * D_MODEL), row), proj, proj, proj,
                  pl.BlockSpec((D_MODEL, D_MODEL), const), pl.BlockSpec((1, D_MODEL), const)],
        out_specs=pl.BlockSpec((tm, D_MODEL), row),
        out_shape=jax.ShapeDtypeStruct((n, D_MODEL), F32),
        compiler_params=_cparams(("parallel",)),
        name="outproj",
    )(x2d, oa, ob, oc, ga, gb, gc, gm, wl["proj_a"], wl["proj_b"], wl["proj_c"], wl["w_out"],
      wl["g_post"])


def _layer_weights(l, w_cat, p):
    row = lambda a: a[l][None, :]
    zeros = jnp.zeros((LORA, BRANCH_W), BF16)
    return {
        "w_cat": w_cat[l],
        "g_pre": row(p["norm_pre"]), "g_post": row(p["norm_post"]),
        "bf_pad": jnp.pad(p["b_forget"][l], (0, LANES - HEADS))[None, :],
        "mu": row(p["rw_mu"]), "w0": row(p["rw_w0"]), "a0": row(p["rw_a0"]),
        "w2p": jnp.concatenate([p["rw_w2"][l].astype(BF16), zeros], axis=0),
        "a2p": jnp.concatenate([zeros, p["rw_a2"][l].astype(BF16)], axis=0),
        "kk": row(p["rw_kk"]), "ka": row(p["rw_ka"]), "rk": row(p["rw_rk"]),
        "lnw": row(p["rw_lnw"]), "lnb": row(p["rw_lnb"]),
        "pool_w": p["pool_w"][l].astype(BF16), "pool_scale": row(p["pool_scale"]),
        "proj_a": p["proj_a"][l].astype(BF16), "proj_b": p["proj_b"][l].astype(BF16),
        "proj_c": p["proj_c"][l].astype(BF16), "w_out": p["w_out"][l].astype(BF16),
    }


def _reorder_w_in(w_in):
    a = RW_SHIFT
    segs = {"z_rw": (0, a), "g_rw": (a, a + 512), "u": (a + 512, a + 1024), "g_pool": (a + 1024, a + 1536),
            "qkv": (a + 1536, a + 3072), "f": (a + 3072, a + 3080), "g_fox": (a + 3080, a + 3592),
            "merge": (a + 3592, a + 3592 + 3 * D_MODEL)}
    order = ["merge", "qkv", "g_rw", "g_pool", "g_fox", "u", "z_rw", "f"]
    cols = [w_in[..., segs[s][0]:segs[s][1]].astype(BF16) for s in order]
    cols.append(jnp.zeros(w_in.shape[:-1] + (LANES - HEADS,), BF16))
    return jnp.concatenate(cols, axis=-1)


def _prompt_layer(x2d, wl, layer, kv_bufs, n_seq, t_seq):
    (gm, qb, kt_buf, vt_buf, kb, vb, grw, gpool, gfox, u, zrw, lf) = _inproj(
        x2d, wl["g_pre"], wl["w_cat"], wl["bf_pad"], 256, HEAD_DIM ** -0.5 * LOG2E, kv_bufs, layer)
    zeros_last = jnp.zeros((n_seq, 1, RW_SHIFT), F32)
    pre = _rwkv_pre(zrw, zeros_last, wl, 512, t_seq)
    o_rw, s_new = _rwkv_chunk(pre, wl["lnw"], wl["lnb"], n_seq, t_seq, 8)
    o_pool, pool_buf = _pool_prompt(u, wl["pool_w"], wl["pool_scale"], n_seq, t_seq)
    c, ct = _fox_cum(lf, n_seq, t_seq)
    o_fox = _fox_prompt(qb, kb, vb, c, ct, n_seq, t_seq, 512)
    y = _outproj(x2d, o_rw, o_pool, o_fox, grw, gpool, gfox, gm, wl, 512)
    shift = zrw.reshape(n_seq, t_seq, RW_SHIFT)[:, -1]
    state = (s_new, shift, pool_buf, lf[:, :HEADS].reshape(n_seq, t_seq, HEADS))
    return y, state, (kt_buf, vt_buf)


def _sample_layer(x2d, wl, layer, n_seq, t_new, state_bl, st_shift, st_pool, caches, page_table):
    (gm, qb, k, v, _, _, grw, gpool, gfox, u, zrw, lf) = _inproj(
        x2d, wl["g_pre"], wl["w_cat"], wl["bf_pad"], 256, HEAD_DIM ** -0.5)
    tm = lambda a: jnp.transpose(a, (1, 0, 2))
    z3 = zrw.reshape(n_seq, t_new, RW_SHIFT)
    o_rw_tm, s_new_bl = _rwkv_sample(layer, tm(z3), st_shift, wl, state_bl)
    o_rw = tm(o_rw_tm).reshape(n_seq * t_new, BRANCH_W)
    s_new = jnp.transpose(s_new_bl, (3, 0, 1, 2))
    past = page_table.shape[1] * PAGE
    o_pool_tm, pool_buf_tm = _pool_sample(tm(st_pool), tm(u.reshape(n_seq, t_new, BRANCH_W)),
                                          wl["pool_w"], wl["pool_scale"], past)
    o_pool = tm(o_pool_tm).reshape(n_seq * t_new, BRANCH_W)
    cache_k, cache_v, cache_logf_t = caches
    tri = lambda a, w: a.reshape(n_seq, t_new, w)
    lf3 = lf[:, :HEADS].reshape(n_seq, t_new, HEADS)
    o_fox3 = _fox_sample(layer, page_table, tri(qb, BRANCH_W), tri(k, BRANCH_W), tri(v, BRANCH_W),
                         jnp.transpose(lf3, (0, 2, 1)), cache_k, cache_v, cache_logf_t)
    y = _outproj(x2d, o_rw, o_pool, o_fox3.reshape(n_seq * t_new, BRANCH_W), grw, gpool, gfox, gm,
                 wl, 256)
    state = (s_new, z3[:, -1], tm(pool_buf_tm),
             k.reshape(n_seq, t_new, HEADS, HEAD_DIM), v.reshape(n_seq, t_new, HEADS, HEAD_DIM), lf3)
    return y, state


def kernel(x_prompt, x_sample, state_rwkv, state_shift, state_pool, cache_k, cache_v, cache_logf, page_table, norm_pre, norm_post, w_in, b_forget, rw_mu, rw_w0, rw_w2, rw_a0, rw_a2, rw_kk, rw_ka, rw_rk, rw_lnw, rw_lnb, pool_w, pool_scale, proj_a, proj_b, proj_c, w_out):
    depth = w_in.shape[0]
    bp, sp, _ = x_prompt.shape
    bs, ts, _ = x_sample.shape
    params = dict(norm_pre=norm_pre, norm_post=norm_post, b_forget=b_forget, rw_mu=rw_mu, rw_w0=rw_w0,
                  rw_w2=rw_w2, rw_a0=rw_a0, rw_a2=rw_a2, rw_kk=rw_kk, rw_ka=rw_ka, rw_rk=rw_rk,
                  rw_lnw=rw_lnw, rw_lnb=rw_lnb, pool_w=pool_w, pool_scale=pool_scale, proj_a=proj_a,
                  proj_b=proj_b, proj_c=proj_c, w_out=w_out)
    w_cat = _reorder_w_in(w_in)
    n_pool = cache_k.shape[1]
    keys_minor = lambda c: jnp.transpose(c, (0, 1, 3, 4, 2)).reshape(depth, n_pool, BRANCH_W, PAGE)
    caches = (keys_minor(cache_k), keys_minor(cache_v), jnp.transpose(cache_logf, (0, 1, 3, 2)))
    state_bl = jnp.transpose(state_rwkv, (0, 2, 3, 4, 1))
    hp = x_prompt.reshape(bp * sp, D_MODEL)
    hs = x_sample.reshape(bs * ts, D_MODEL)
    kv_bufs = (jnp.zeros((depth, bp, BRANCH_W, sp), F32), jnp.zeros((depth, bp, BRANCH_W, sp), F32))
    st_p, st_s = [], []
    for l in range(depth):
        wl = _layer_weights(l, w_cat, params)
        hp, sp_state, kv_bufs = _prompt_layer(hp, wl, l, kv_bufs, bp, sp)
        hs, ss_state = _sample_layer(hs, wl, l, bs, ts, state_bl, state_shift[l], state_pool[l],
                                     caches, page_table)
        st_p.append(sp_state)
        st_s.append(ss_state)
    stack = lambda states, i: jnp.stack([s[i] for s in states])
    rows_major = lambda buf: jnp.transpose(buf.reshape(depth, bp, HEADS, HEAD_DIM, sp), (0, 1, 4, 2, 3))
    return (hp.reshape(bp, sp, D_MODEL), hs.reshape(bs, ts, D_MODEL),
            stack(st_p, 0), stack(st_s, 0), stack(st_p, 1), stack(st_s, 1), stack(st_p, 2), stack(st_s, 2),
            rows_major(kv_bufs[0]), stack(st_s, 3), rows_major(kv_bufs[1]), stack(st_s, 4),
            stack(st_p, 3), stack(st_s, 5))
```

```python
import functools

import jax
import jax.numpy as jnp
from jax import lax
from jax.experimental import pallas as pl
from jax.experimental.pallas import tpu as pltpu

F32 = jnp.float32
BF16 = jnp.bfloat16
HIGHEST = lax.Precision.HIGHEST

D_MODEL = 1024
HEADS = 8
HEAD_DIM = 64
BRANCH_W = HEADS * HEAD_DIM
LORA = 64
RW_SHIFT = 3 * BRANCH_W + 2 * LORA
POOL_WINDOWS = (2, 4, 8, 16)
POOL_GW = 128
POOL_BUF = 15
PAGE = 128
CHUNK = 64
CPI = 4
LANES = 128
EPS = 1e-6
LNX_EPS = 64e-5
LOG2E = 1.4426950408889634
NEG_EXP_M_HALF = -0.6065306597126334

C_MERGE = 0
C_QKV = 3 * D_MODEL
C_GRW = C_QKV + 3 * BRANCH_W
C_GPOOL = C_GRW + BRANCH_W
C_GFOX = C_GPOOL + BRANCH_W
C_UPOOL = C_GFOX + BRANCH_W
C_ZRW = C_UPOOL + BRANCH_W
C_F = C_ZRW + RW_SHIFT
W_COLS = C_F + LANES

VMEM_LIMIT = 56 * 1024 * 1024


def _cparams(sem):
    return pltpu.CompilerParams(dimension_semantics=sem, vmem_limit_bytes=VMEM_LIMIT)


def _sigmoid(x):
    return 1.0 / (1.0 + jnp.exp(-x))


def _log_sigmoid(x):
    return jnp.minimum(x, 0.0) - jnp.log1p(jnp.exp(-jnp.abs(x)))


def _dot(a, b):
    return jnp.dot(a.astype(BF16), b.astype(BF16), preferred_element_type=F32)


def _dot_nt(a, b):
    return lax.dot_general(a.astype(BF16), b.astype(BF16), (((1,), (1,)), ((), ())),
                           preferred_element_type=F32)


def _dot_tn(a, b):
    return lax.dot_general(a.astype(BF16), b.astype(BF16), (((0,), (0,)), ((), ())),
                           preferred_element_type=F32)


def _inproj_kernel(*refs, q_scale, keys_minor):
    x_ref, g_ref, w_ref, bf_ref = refs[:4]
    outs = refs[6:] if keys_minor else refs[4:]
    (gm_ref, q_ref, k_ref, v_ref, kb_ref, vb_ref, grw_ref, gpool_ref, gfox_ref, u_ref, zrw_ref,
     lf_ref) = outs
    x = x_ref[...]
    ms = jnp.mean(x * x, axis=-1, keepdims=True)
    xn = (x * lax.rsqrt(ms + EPS) * g_ref[...]).astype(BF16)

    def mm(c0, width):
        return jnp.dot(xn, w_ref[:, c0:c0 + width], preferred_element_type=F32)

    for i in range(3):
        gm_ref[:, i * D_MODEL:(i + 1) * D_MODEL] = _sigmoid(mm(C_MERGE + i * D_MODEL, D_MODEL))
    q_ref[...] = (mm(C_QKV, BRANCH_W) * q_scale).astype(BF16)
    k = mm(C_QKV + BRANCH_W, BRANCH_W)
    k_ref[...] = k.T if keys_minor else k
    kb_ref[...] = k.astype(BF16)
    v = mm(C_QKV + 2 * BRANCH_W, BRANCH_W)
    v_ref[...] = v.T if keys_minor else v
    vb_ref[...] = v.astype(BF16)
    for ref, c0 in ((grw_ref, C_GRW), (gpool_ref, C_GPOOL), (gfox_ref, C_GFOX)):
        g = mm(c0, BRANCH_W)
        ref[...] = g * _sigmoid(g)
    u_ref[...] = mm(C_UPOOL, BRANCH_W)
    zrw_ref[:, 0:3 * BRANCH_W] = mm(C_ZRW, 3 * BRANCH_W)
    zrw_ref[:, 3 * BRANCH_W:RW_SHIFT] = mm(C_ZRW + 3 * BRANCH_W, 2 * LORA)
    f = mm(C_F, LANES) + bf_ref[...]
    head_lane = lax.broadcasted_iota(jnp.int32, f.shape, 1) < HEADS
    lf_ref[...] = jnp.where(head_lane, _log_sigmoid(f), 0.0)


def _inproj(x2d, g_pre, w_cat, bf_pad, tm, q_scale, kv_bufs=None, layer=0):
    n = x2d.shape[0]
    row = lambda i: (i, 0)
    const = lambda i: (0, 0)
    widths = [(3 * D_MODEL, F32), (BRANCH_W, BF16), (BRANCH_W, F32), (BRANCH_W, F32),
              (BRANCH_W, BF16), (BRANCH_W, BF16), (BRANCH_W, F32), (BRANCH_W, F32),
              (BRANCH_W, F32), (BRANCH_W, F32), (RW_SHIFT, F32), (LANES, F32)]
    in_specs = [pl.BlockSpec((tm, D_MODEL), row),
                pl.BlockSpec((1, D_MODEL), const),
                pl.BlockSpec((D_MODEL, W_COLS), const, pipeline_mode=pl.Buffered(1)),
                pl.BlockSpec((1, LANES), const)]
    out_specs = [pl.BlockSpec((tm, w), row) for w, _ in widths]
    out_shape = [jax.ShapeDtypeStruct((n, w), dt) for w, dt in widths]
    args = [x2d, g_pre, w_cat, bf_pad]
    aliases = {}
    if kv_bufs is not None:
        tiles = kv_bufs[0].shape[3] // tm
        kv_spec = pl.BlockSpec((None, None, BRANCH_W, tm), lambda i: (layer, i // tiles, 0, i % tiles))
        for j, buf in enumerate(kv_bufs):
            in_specs.append(pl.BlockSpec(memory_space=pl.ANY))
            args.append(buf)
            out_specs[2 + j] = kv_spec
            out_shape[2 + j] = jax.ShapeDtypeStruct(buf.shape, buf.dtype)
            aliases[4 + j] = 2 + j
    return pl.pallas_call(
        functools.partial(_inproj_kernel, q_scale=q_scale, keys_minor=kv_bufs is not None),
        grid=(n // tm,),
        in_specs=in_specs,
        out_specs=out_specs,
        out_shape=out_shape,
        input_output_aliases=aliases,
        compiler_params=_cparams(("parallel",)),
        name="inproj",
    )(*args)


def _seg_sum(x, lo_mask):
    s_lo = jnp.sum(jnp.where(lo_mask, x, 0.0), axis=-1, keepdims=True)
    s_hi = jnp.sum(jnp.where(lo_mask, 0.0, x), axis=-1, keepdims=True)
    return jnp.where(lo_mask, s_lo, s_hi)


def _seg_sum_wide(x):
    tt = x.shape[0]
    lo_mask = lax.broadcasted_iota(jnp.int32, (tt, LANES), 1) < HEAD_DIM
    parts = [_seg_sum(x[:, p * LANES:(p + 1) * LANES], lo_mask) for p in range(x.shape[1] // LANES)]
    return jnp.concatenate(parts, axis=-1)


def _rwkv_pre_kernel(z_ref, zp_ref, zl_ref, mu_ref, w0_ref, w2_ref, a0_ref, a2_ref, kk_ref, ka_ref,
                     rk_ref, at_ref, rt_ref, bt_ref, kt_ref, vb_ref, bonus_ref, pc_ref,
                     *, tt, t_seq):
    i = pl.program_id(0)
    z = z_ref[...]
    starts_seq = (i * tt) % t_seq == 0
    prev_row = jnp.where(starts_seq, zl_ref[...], zp_ref[7:8, :])
    row = lax.broadcasted_iota(jnp.int32, (tt, 1), 0)
    z_prev = jnp.where(row == 0, prev_row, pltpu.roll(z, 1, axis=0))
    zs = z + (z_prev - z) * mu_ref[...]
    w3 = BRANCH_W
    r = zs[:, 0:w3]
    k = zs[:, w3:2 * w3]
    v = zs[:, 2 * w3:3 * w3]
    lora = zs[:, 3 * w3:RW_SHIFT]
    wx = w0_ref[...] + jnp.dot(jnp.tanh(lora).astype(BF16), w2_ref[...], preferred_element_type=F32)
    logd = _sigmoid(wx) * NEG_EXP_M_HALF
    a = _sigmoid(a0_ref[...] + jnp.dot(lora.astype(BF16), a2_ref[...], preferred_element_type=F32))
    kk = k * kk_ref[...]
    k2 = k * (1.0 + (a - 1.0) * ka_ref[...])
    kk = kk / jnp.maximum(jnp.sqrt(_seg_sum_wide(kk * kk)), 1e-12)
    bonus_ref[...] = _seg_sum_wide(r * k2 * rk_ref[...]) * v
    ri = lax.broadcasted_iota(jnp.int32, (CHUNK, CHUNK), 0)
    ci = lax.broadcasted_iota(jnp.int32, (CHUNK, CHUNK), 1)
    tri = jnp.where(ci <= ri, 1.0, 0.0).astype(F32)
    cl = jnp.concatenate(
        [jnp.dot(tri, logd[c * CHUNK:(c + 1) * CHUNK, :], preferred_element_type=F32, precision=HIGHEST)
         for c in range(tt // CHUNK)], axis=0)
    e_neg = jnp.exp(-cl)
    at_ref[...] = (-kk * jnp.exp(cl - logd)).astype(BF16)
    rt_ref[...] = (r * jnp.exp(cl)).astype(BF16)
    bt_ref[...] = (kk * a * e_neg).astype(BF16)
    kt_ref[...] = (k2 * e_neg).astype(BF16)
    vb_ref[...] = v.astype(BF16)
    for c in range(tt // CHUNK):
        last = cl[(c + 1) * CHUNK - 1:(c + 1) * CHUNK, :]
        pc_ref[c * 8:(c + 1) * 8, :] = jnp.broadcast_to(jnp.exp(last), (8, BRANCH_W))


def _rwkv_pre(z, z_last, wl, tt, t_seq):
    n = z.shape[0]
    row = lambda i: (i, 0)
    const = lambda i: (0, 0)
    vec = lambda w: pl.BlockSpec((1, w), const)
    nb8 = tt // 8
    kern = functools.partial(_rwkv_pre_kernel, tt=tt, t_seq=t_seq)
    wide = lambda dt: jax.ShapeDtypeStruct((n, BRANCH_W), dt)
    return pl.pallas_call(
        kern,
        grid=(n // tt,),
        in_specs=[pl.BlockSpec((tt, RW_SHIFT), row),
                  pl.BlockSpec((8, RW_SHIFT), lambda i: (jnp.maximum(i * nb8 - 1, 0), 0)),
                  pl.BlockSpec((None, 1, RW_SHIFT), lambda i: ((i * tt) // t_seq, 0, 0)),
                  vec(RW_SHIFT), vec(BRANCH_W),
                  pl.BlockSpec((LANES, BRANCH_W), const),
                  vec(BRANCH_W),
                  pl.BlockSpec((LANES, BRANCH_W), const),
                  vec(BRANCH_W), vec(BRANCH_W), vec(BRANCH_W)],
        out_specs=[pl.BlockSpec((tt, BRANCH_W), row)] * 6
                  + [pl.BlockSpec((tt // CHUNK * 8, BRANCH_W), row)],
        out_shape=[wide(BF16), wide(BF16), wide(BF16), wide(BF16), wide(BF16), wide(F32),
                   jax.ShapeDtypeStruct((n // CHUNK * 8, BRANCH_W), F32)],
        compiler_params=_cparams(("parallel",)),
        name="rwkv_pre",
    )(z, z, z_last, wl["mu"], wl["w0"], wl["w2p"], wl["a0"], wl["a2p"], wl["kk"], wl["ka"], wl["rk"])


def _rwkv_chunk_kernel(at_ref, rt_ref, bt_ref, kt_ref, vb_ref, bonus_ref, pc_ref, gate_ref, lnw_ref,
                       lnb_ref, o_ref, sout_ref, s_scr, *, cpb):
    g = pl.program_id(1)
    c2 = 2 * CHUNK
    pairs = HEADS // 2

    @pl.when(g == 0)
    def _():
        s_scr[...] = jnp.zeros(s_scr.shape, F32)

    lane = lax.broadcasted_iota(jnp.int32, (CHUNK, LANES), 1)
    lo = lane < HEAD_DIM
    ri = lax.broadcasted_iota(jnp.int32, (c2, c2), 0)
    ci = lax.broadcasted_iota(jnp.int32, (c2, c2), 1)
    same = (ri // CHUNK) == (ci // CHUNK)
    strict = same & (ci < ri)
    incl = same & (ci <= ri)
    eye = jnp.where(ri == ci, 1.0, 0.0).astype(F32)

    def stack2(x):
        x = x.astype(F32)
        return jnp.concatenate([jnp.where(lo, x, 0.0), jnp.where(lo, 0.0, x)], axis=0)

    def body(i, carry):
        rows = [pl.ds(pl.multiple_of((i * CPI + cc) * CHUNK, CHUNK), CHUNK) for cc in range(CPI)]
        pc_row = [pl.ds(pl.multiple_of((i * CPI + cc) * 8, 8), 1) for cc in range(CPI)]
        lanes = [slice(p * LANES, (p + 1) * LANES) for p in range(pairs)]
        units = [(rows[cc], lanes[p]) for cc in range(CPI) for p in range(pairs)]
        each = lambda f, *xs: [f(*args) for args in zip(*xs)]
        a_s = [stack2(at_ref[rw, ln]) for rw, ln in units]
        r_s = [stack2(rt_ref[rw, ln]) for rw, ln in units]
        b_s = [stack2(bt_ref[rw, ln]) for rw, ln in units]
        k_s = [stack2(kt_ref[rw, ln]) for rw, ln in units]
        vp = [vb_ref[rw, ln].astype(F32) for rw, ln in units]
        v_s = [jnp.concatenate([x[:, :HEAD_DIM], x[:, HEAD_DIM:]], axis=0) for x in vp]
        pc = [pc_ref[pc_row[cc], lanes[p]] for cc in range(CPI) for p in range(pairs)]
        nmat = each(lambda a, b: jnp.where(strict, _dot_nt(a, b), 0.0), a_s, b_s)
        lak = each(lambda a, k: jnp.where(strict, _dot_nt(a, k), 0.0), a_s, k_s)
        mrb = each(lambda r, b: jnp.where(incl, _dot_nt(r, b), 0.0), r_s, b_s)
        mrk = each(lambda r, k: jnp.where(incl, _dot_nt(r, k), 0.0), r_s, k_s)
        lakv = each(_dot, lak, v_s)
        mrkv = each(_dot, mrk, v_s)
        vtk = each(_dot_tn, v_s, k_s)
        tinv = [eye + n for n in nmat]
        pw = nmat
        for _ in range(5):
            pw = each(_dot, pw, pw)
            tinv = each(lambda t, q: t + _dot(t, q), tinv, pw)
        a_hat = each(_dot, tinv, a_s)
        u_hat = each(_dot, tinv, lakv)
        r_hat = each(lambda r, m, a: r + _dot(m, a), r_s, mrb, a_hat)
        y_hat = each(lambda m, u, y: _dot(m, u) + y, mrb, u_hat, mrkv)
        gmat = each(lambda a, b, x: (eye + _dot_tn(a, b)) * x, a_hat, b_s, pc)
        hmat = each(lambda u, b, y, x: (_dot_tn(u, b) + y) * x, u_hat, b_s, vtk, pc)
        s = [s_scr[p] for p in range(pairs)]
        for cc in range(CPI):
            sel = slice(cc * pairs, (cc + 1) * pairs)
            y_s = each(lambda r, st, y: _dot_nt(r, st) + y, r_hat[sel], s, y_hat[sel])
            s = each(lambda st, gm, hm: _dot(st, gm) + hm, s, gmat[sel], hmat[sel])
            for p in range(pairs):
                y = jnp.concatenate([y_s[p][:CHUNK], y_s[p][CHUNK:]], axis=1)
                mean = _seg_sum(y, lo) * (1.0 / HEAD_DIM)
                yc = y - mean
                var = _seg_sum(yc * yc, lo) * (1.0 / HEAD_DIM)
                ln = lanes[p]
                o_rw = yc * lax.rsqrt(var + LNX_EPS) * lnw_ref[:, ln] + lnb_ref[:, ln] + bonus_ref[rows[cc], ln]
                o_ref[rows[cc], ln] = (o_rw * gate_ref[rows[cc], ln]).astype(BF16)
        for p in range(pairs):
            s_scr[p] = s[p]
        return carry

    lax.fori_loop(0, cpb // CPI, body, 0)

    @pl.when(g == pl.num_programs(1) - 1)
    def _():
        for p in range(pairs):
            s = s_scr[p]
            sout_ref[2 * p] = s[:, :HEAD_DIM]
            sout_ref[2 * p + 1] = s[:, HEAD_DIM:]


def _rwkv_chunk(pre, gate, lnw, lnb, n_seq, t_seq, cpb):
    at, rt, bt, kt, vb, bonus, pc = pre
    n = at.shape[0]
    ng = t_seq // (cpb * CHUNK)
    rows = cpb * CHUNK
    tile = lambda b, g: (b * ng + g, 0)
    const = lambda b, g: (0, 0)
    state = pl.BlockSpec((None, HEADS, HEAD_DIM, HEAD_DIM), lambda b, g: (b, 0, 0, 0))
    kern = functools.partial(_rwkv_chunk_kernel, cpb=cpb)
    return pl.pallas_call(
        kern,
        grid=(n_seq, ng),
        in_specs=[pl.BlockSpec((rows, BRANCH_W), tile)] * 6
                 + [pl.BlockSpec((cpb * 8, BRANCH_W), tile),
                    pl.BlockSpec((rows, BRANCH_W), tile),
                    pl.BlockSpec((1, BRANCH_W), const),
                    pl.BlockSpec((1, BRANCH_W), const)],
        out_specs=[pl.BlockSpec((rows, BRANCH_W), tile), state],
        out_shape=[jax.ShapeDtypeStruct((n, BRANCH_W), BF16),
                   jax.ShapeDtypeStruct((n_seq, HEADS, HEAD_DIM, HEAD_DIM), F32)],
        scratch_shapes=[pltpu.VMEM((HEADS // 2, HEAD_DIM, LANES), F32)],
        compiler_params=_cparams(("parallel", "arbitrary")),
        name="rwkv_chunk",
    )(at, rt, bt, kt, vb, bonus, pc, gate, lnw, lnb)


def _rwkv_sample_kernel(zr_ref, zk_ref, zv_ref, zl_ref, lr_ref, lk_ref, lv_ref, ll_ref,
                        mur_ref, muk_ref, muv_ref, mul_ref, w0_ref, w2_ref, a0_ref, a2_ref, kk_ref,
                        ka_ref, rk_ref, lnw_ref, lnb_ref, s_ref, o_ref, sout_ref, tr_ref, yt_ref,
                        *, t_new):
    n_b = zr_ref.shape[1]
    lo = lax.broadcasted_iota(jnp.int32, (n_b, LANES), 1) < HEAD_DIM

    def shifted(cur_ref, last_ref, mu_ref, t):
        cur = cur_ref[t]
        prev = last_ref[...] if t == 0 else cur_ref[t - 1]
        return cur + (prev - cur) * mu_ref[...]

    for t in range(t_new):
        r = shifted(zr_ref, lr_ref, mur_ref, t)
        k = shifted(zk_ref, lk_ref, muk_ref, t)
        v = shifted(zv_ref, lv_ref, muv_ref, t)
        lora = shifted(zl_ref, ll_ref, mul_ref, t)
        wx = w0_ref[...] + jnp.dot(jnp.tanh(lora).astype(BF16), w2_ref[...], preferred_element_type=F32)
        logd = _sigmoid(wx) * NEG_EXP_M_HALF
        a = _sigmoid(a0_ref[...] + jnp.dot(lora.astype(BF16), a2_ref[...], preferred_element_type=F32))
        kk = k * kk_ref[...]
        k2 = k * (1.0 + (a - 1.0) * ka_ref[...])
        kk = kk / jnp.maximum(jnp.sqrt(_seg_sum(kk * kk, lo)), 1e-12)
        o_ref[t] = _seg_sum(r * k2 * rk_ref[...], lo) * v
        for i, x in enumerate((-kk, jnp.exp(logd), kk * a, k2, r, v)):
            tr_ref[t, i] = x.T

    for hh in range(2):
        h0 = hh * HEAD_DIM

        def row_step(vi, carry):
            s_v = s_ref[hh, vi]
            for t in range(t_new):
                nkk = tr_ref[t, 0, h0:h0 + HEAD_DIM, :]
                dec = tr_ref[t, 1, h0:h0 + HEAD_DIM, :]
                kka = tr_ref[t, 2, h0:h0 + HEAD_DIM, :]
                k2 = tr_ref[t, 3, h0:h0 + HEAD_DIM, :]
                r = tr_ref[t, 4, h0:h0 + HEAD_DIM, :]
                vv = tr_ref[t, 5, pl.ds(h0 + vi, 1), :]
                sa = jnp.sum(s_v * nkk, axis=0, keepdims=True)
                s_v = s_v * dec + sa * kka + vv * k2
                yt_ref[t, pl.ds(h0 + vi, 1), :] = jnp.sum(s_v * r, axis=0, keepdims=True)
            sout_ref[hh, vi] = s_v
            return carry

        lax.fori_loop(0, HEAD_DIM, row_step, 0, unroll=4)

    for t in range(t_new):
        y = yt_ref[t].T
        mean = _seg_sum(y, lo) * (1.0 / HEAD_DIM)
        yc = y - mean
        var = _seg_sum(yc * yc, lo) * (1.0 / HEAD_DIM)
        o_ref[t] = o_ref[t] + yc * lax.rsqrt(var + LNX_EPS) * lnw_ref[...] + lnb_ref[...]


def _rwkv_sample(layer, zt, z_last, wl, state_bl):
    t_new, n_b, _ = zt.shape
    pairs = HEADS // 2
    lora_tile = 3 * pairs
    seg = lambda s: pl.BlockSpec((t_new, n_b, LANES), lambda p, s=s: (0, 0, s * pairs + p))
    seg_last = lambda s: pl.BlockSpec((n_b, LANES), lambda p, s=s: (0, s * pairs + p))
    seg_mu = lambda s: pl.BlockSpec((1, LANES), lambda p, s=s: (0, s * pairs + p))
    vec = pl.BlockSpec((1, LANES), lambda p: (0, p))
    lora_w = pl.BlockSpec((LANES, LANES), lambda p: (0, p))
    kern = functools.partial(_rwkv_sample_kernel, t_new=t_new)
    return pl.pallas_call(
        kern,
        grid=(pairs,),
        in_specs=[seg(0), seg(1), seg(2),
                  pl.BlockSpec((t_new, n_b, LANES), lambda p: (0, 0, lora_tile)),
                  seg_last(0), seg_last(1), seg_last(2),
                  pl.BlockSpec((n_b, LANES), lambda p: (0, lora_tile)),
                  seg_mu(0), seg_mu(1), seg_mu(2),
                  pl.BlockSpec((1, LANES), lambda p: (0, lora_tile)),
                  vec, lora_w, vec, lora_w, vec, vec, vec, vec, vec,
                  pl.BlockSpec((None, 2, HEAD_DIM, HEAD_DIM, n_b), lambda p: (layer, p, 0, 0, 0))],
        out_specs=[pl.BlockSpec((t_new, n_b, LANES), lambda p: (0, 0, p)),
                   pl.BlockSpec((2, HEAD_DIM, HEAD_DIM, n_b), lambda p: (p, 0, 0, 0))],
        out_shape=[jax.ShapeDtypeStruct((t_new, n_b, BRANCH_W), F32),
                   jax.ShapeDtypeStruct((HEADS, HEAD_DIM, HEAD_DIM, n_b), F32)],
        scratch_shapes=[pltpu.VMEM((t_new, 6, LANES, n_b), F32), pltpu.VMEM((t_new, LANES, n_b), F32)],
        compiler_params=_cparams(("parallel",)),
        name="rwkv_sample",
    )(zt, zt, zt, zt, z_last, z_last, z_last, z_last, wl["mu"], wl["mu"], wl["mu"], wl["mu"],
      wl["w0"], wl["w2p"], wl["a0"], wl["a2p"], wl["kk"], wl["ka"], wl["rk"], wl["lnw"], wl["lnb"],
      state_bl)


def _pool_prompt_kernel(u_ref, pw_ref, ps_ref, gate_ref, o_ref, buf_ref, ext_ref, *, t_seq, tile):
    pad = POOL_BUF + 1
    ext_ref[0:pad, :] = jnp.zeros((pad, BRANCH_W), F32)
    ext_ref[pad:pad + t_seq, :] = u_ref[...]
    buf_ref[...] = u_ref[t_seq - POOL_BUF:t_seq, :]
    for t0 in range(0, t_seq, tile):
        pos = t0 + lax.broadcasted_iota(jnp.int32, (tile, 1), 0)
        for gi, w in enumerate(POOL_WINDOWS):
            sl = slice(gi * POOL_GW, (gi + 1) * POOL_GW)
            wsum = ext_ref[pad + t0:pad + t0 + tile, sl]
            for j in range(1, w):
                wsum = wsum + ext_ref[pad + t0 - j:pad + t0 - j + tile, sl]
            cnt = jnp.minimum(pos + 1, w).astype(F32)
            d = wsum / cnt - ext_ref[pad + t0:pad + t0 + tile, sl]
            o_pool = _dot(d, pw_ref[gi]) * ps_ref[:, sl]
            o_ref[t0:t0 + tile, sl] = (o_pool * gate_ref[t0:t0 + tile, sl]).astype(BF16)


def _pool_prompt(u, gate, pool_w, pool_scale, n_seq, t_seq):
    kern = functools.partial(_pool_prompt_kernel, t_seq=t_seq, tile=256)
    seq = pl.BlockSpec((t_seq, BRANCH_W), lambda b: (b, 0))
    return pl.pallas_call(
        kern,
        grid=(n_seq,),
        in_specs=[seq,
                  pl.BlockSpec((len(POOL_WINDOWS), POOL_GW, POOL_GW), lambda b: (0, 0, 0)),
                  pl.BlockSpec((1, BRANCH_W), lambda b: (0, 0)),
                  seq],
        out_specs=[seq, pl.BlockSpec((None, POOL_BUF, BRANCH_W), lambda b: (b, 0, 0))],
        out_shape=[jax.ShapeDtypeStruct((n_seq * t_seq, BRANCH_W), BF16),
                   jax.ShapeDtypeStruct((n_seq, POOL_BUF, BRANCH_W), F32)],
        scratch_shapes=[pltpu.VMEM((POOL_BUF + 1 + t_seq, BRANCH_W), F32)],
        compiler_params=_cparams(("parallel",)),
        name="pool_prompt",
    )(u, pool_w, pool_scale, gate)


def _pool_sample_kernel(pre_ref, u_ref, pw_ref, ps_ref, o_ref, buf_ref, *, t_new, pos0):
    def ext_row(i, sl):
        if i < POOL_BUF:
            return pre_ref[i, :, sl]
        return u_ref[i - POOL_BUF, :, sl]

    full = slice(0, BRANCH_W)
    for i in range(POOL_BUF):
        buf_ref[i] = ext_row(i + t_new, full)
    for t in range(t_new):
        for gi, w in enumerate(POOL_WINDOWS):
            sl = slice(gi * POOL_GW, (gi + 1) * POOL_GW)
            cur = ext_row(POOL_BUF + t, sl)
            wsum = cur
            for j in range(1, w):
                wsum = wsum + ext_row(POOL_BUF + t - j, sl)
            cnt = float(min(pos0 + t + 1, w))
            d = wsum / cnt - cur
            o_ref[t, :, sl] = _dot(d, pw_ref[gi]) * ps_ref[:, sl]


def _pool_sample(prefix, u3, pool_w, pool_scale, pos0):
    t_new = u3.shape[0]
    kern = functools.partial(_pool_sample_kernel, t_new=t_new, pos0=pos0)
    full3 = lambda shape: pl.BlockSpec(shape, lambda i: (0, 0, 0))
    return pl.pallas_call(
        kern,
        grid=(1,),
        in_specs=[full3(prefix.shape), full3(u3.shape), full3(pool_w.shape),
                  pl.BlockSpec((1, BRANCH_W), lambda i: (0, 0))],
        out_specs=[full3(u3.shape), full3(prefix.shape)],
        out_shape=[jax.ShapeDtypeStruct(u3.shape, F32), jax.ShapeDtypeStruct(prefix.shape, F32)],
        compiler_params=_cparams(("arbitrary",)),
        name="pool_sample",
    )(prefix, u3, pool_w, pool_scale)


def _fox_cum_kernel(lf_ref, c_ref, ct_ref, *, t_seq, tile):
    ri = lax.broadcasted_iota(jnp.int32, (tile, tile), 0)
    ci = lax.broadcasted_iota(jnp.int32, (tile, tile), 1)
    tri = jnp.where(ci <= ri, 1.0, 0.0).astype(F32)
    e8 = jnp.where(lax.broadcasted_iota(jnp.int32, (HEADS, LANES), 0)
                   == lax.broadcasted_iota(jnp.int32, (HEADS, LANES), 1), 1.0, 0.0).astype(F32)
    carry = jnp.zeros((1, LANES), F32)
    for t0 in range(0, t_seq, tile):
        c = jnp.dot(tri, lf_ref[t0:t0 + tile, :], preferred_element_type=F32, precision=HIGHEST) + carry
        c_ref[t0:t0 + tile, :] = c
        ct_ref[:, t0:t0 + tile] = lax.dot_general(e8, c, (((1,), (1,)), ((), ())),
                                                  preferred_element_type=F32, precision=HIGHEST)
        carry = c[tile - 1:tile, :]


def _fox_cum(lf, n_seq, t_seq):
    kern = functools.partial(_fox_cum_kernel, t_seq=t_seq, tile=256)
    return pl.pallas_call(
        kern,
        grid=(n_seq,),
        in_specs=[pl.BlockSpec((t_seq, LANES), lambda b: (b, 0))],
        out_specs=[pl.BlockSpec((t_seq, LANES), lambda b: (b, 0)),
                   pl.BlockSpec((None, HEADS, t_seq), lambda b: (b, 0, 0))],
        out_shape=[jax.ShapeDtypeStruct((n_seq * t_seq, LANES), F32),
                   jax.ShapeDtypeStruct((n_seq, HEADS, t_seq), F32)],
        compiler_params=_cparams(("parallel",)),
        name="fox_cum",
    )(lf)


def _fox_prompt_kernel(q_ref, k_ref, v_ref, c_ref, ct_ref, gate_ref, o_ref, *, tq):
    p = pl.program_id(1)
    qi = pl.program_id(2)
    q = q_ref[...].astype(F32)
    lane = lax.broadcasted_iota(jnp.int32, (tq, LANES), 1)
    lo = lane < HEAD_DIM
    ri = lax.broadcasted_iota(jnp.int32, (tq, tq), 0)
    ci = lax.broadcasted_iota(jnp.int32, (tq, tq), 1)
    causal = ci <= ri
    c_all = c_ref[...]
    qh = [jnp.where(lo, q, 0.0).astype(BF16), jnp.where(lo, 0.0, q).astype(BF16)]
    cq = [jnp.sum(jnp.where(lane == 2 * p + hh, c_all, 0.0), axis=-1, keepdims=True) * LOG2E
          for hh in range(2)]

    def block(j, carry, masked):
        cols = pl.ds(pl.multiple_of(j * tq, tq), tq)
        k = k_ref[cols, :]
        v = v_ref[cols, :]
        heads = range(2)
        ck = [ct_ref[pl.ds(2 * p + hh, 1), cols] * LOG2E for hh in heads]
        s = [lax.dot_general(qh[hh], k, (((1,), (1,)), ((), ())), preferred_element_type=F32) - ck[hh]
             for hh in heads]
        if masked:
            s = [jnp.where(causal, x, -jnp.inf) for x in s]
        m_new = [jnp.maximum(carry[hh][0], jnp.max(s[hh], axis=-1, keepdims=True) + cq[hh]) for hh in heads]
        pr = [jnp.exp2(s[hh] - (m_new[hh] - cq[hh])) for hh in heads]
        alpha = [jnp.exp2(carry[hh][0] - m_new[hh]) for hh in heads]
        l = [alpha[hh] * carry[hh][1] + jnp.sum(pr[hh], axis=-1, keepdims=True) for hh in heads]
        acc = [alpha[hh] * carry[hh][2] + jnp.dot(pr[hh].astype(BF16), v, preferred_element_type=F32)
               for hh in heads]
        return tuple((m_new[hh], l[hh], acc[hh]) for hh in heads)

    one = (jnp.full((tq, 1), -jnp.inf, F32), jnp.zeros((tq, 1), F32), jnp.zeros((tq, LANES), F32))
    carry = lax.fori_loop(0, qi, lambda j, c: block(j, c, False), (one, one))
    (_, l0, acc0), (_, l1, acc1) = block(qi, carry, True)
    o_ref[...] = (jnp.where(lo, acc0 / l0, acc1 / l1) * gate_ref[...]).astype(BF16)


def _fox_prompt(qb, kb, vb, c, ct, gate, n_seq, t_seq, tq):
    nq = t_seq // tq
    pairs = HEADS // 2
    kern = functools.partial(_fox_prompt_kernel, tq=tq)
    return pl.pallas_call(
        kern,
        grid=(n_seq, pairs, nq),
        in_specs=[pl.BlockSpec((tq, LANES), lambda b, p, i: (b * nq + i, p)),
                  pl.BlockSpec((t_seq, LANES), lambda b, p, i: (b, p)),
                  pl.BlockSpec((t_seq, LANES), lambda b, p, i: (b, p)),
                  pl.BlockSpec((tq, LANES), lambda b, p, i: (b * nq + i, 0)),
                  pl.BlockSpec((None, HEADS, t_seq), lambda b, p, i: (b, 0, 0)),
                  pl.BlockSpec((tq, LANES), lambda b, p, i: (b * nq + i, p))],
        out_specs=pl.BlockSpec((tq, LANES), lambda b, p, i: (b * nq + i, p)),
        out_shape=jax.ShapeDtypeStruct((n_seq * t_seq, BRANCH_W), BF16),
        compiler_params=_cparams(("parallel", "parallel", "arbitrary")),
        name="fox_prompt",
    )(qb, kb, vb, c, ct, gate)


def _fox_sample_kernel(pt_ref, q_ref, kn_ref, vn_ref, lfn_ref, *refs, n_pages, t_new):
    del pt_ref
    k_pages = refs[0:n_pages]
    v_pages = refs[n_pages:2 * n_pages]
    lf_pages = refs[2 * n_pages:3 * n_pages]
    o_ref = refs[3 * n_pages]
    past = n_pages * PAGE
    rows = t_new * HEADS
    nt = (((1,), (1,)), ((), ()))
    lf_t = jnp.concatenate([lf_pages[i][...] for i in range(n_pages)], axis=1)
    pos = lax.broadcasted_iota(jnp.int32, (HEADS, past), 1)
    x = lf_t
    sh = 1
    while sh < past:
        x = x + jnp.where(pos + sh < past, pltpu.roll(x, past - sh, axis=1), 0.0)
        sh *= 2
    suffix = x - lf_t
    lfn = lfn_ref[...]
    cn = [lfn[:, 0:1]]
    for t in range(1, t_new):
        cn.append(cn[-1] + lfn[:, t:t + 1])
    q = q_ref[...].astype(F32)
    head_of_lane = lax.broadcasted_iota(jnp.int32, (HEADS, BRANCH_W), 1) // HEAD_DIM
    head_mask = head_of_lane == lax.broadcasted_iota(jnp.int32, (HEADS, BRANCH_W), 0)
    qbd = jnp.concatenate(
        [jnp.where(head_mask, jnp.broadcast_to(q[t:t + 1, :], (HEADS, BRANCH_W)), 0.0)
         for t in range(t_new)], axis=0)
    qbd_bf = qbd.astype(BF16)
    s_past = jnp.concatenate(
        [jnp.dot(qbd_bf, k_pages[i][...].astype(BF16), preferred_element_type=F32)
         for i in range(n_pages)], axis=1)
    s_past = s_past + jnp.concatenate([suffix + cn[t] for t in range(t_new)], axis=0)
    row_t = lax.broadcasted_iota(jnp.int32, (rows, 1), 0) // HEADS
    kn = kn_ref[...]
    vn = vn_ref[...]
    s_new = []
    for j in range(t_new):
        dots = jnp.sum(qbd * kn[j:j + 1, :], axis=-1, keepdims=True)
        bias = jnp.concatenate([cn[t] - cn[j] for t in range(t_new)], axis=0)
        s_new.append(jnp.where(row_t >= j, dots + bias, -jnp.inf))
    m = jnp.max(s_past, axis=-1, keepdims=True)
    for j in range(t_new):
        m = jnp.maximum(m, s_new[j])
    p_past = jnp.exp(s_past - m)
    l = jnp.sum(p_past, axis=-1, keepdims=True)
    o = jnp.zeros((rows, BRANCH_W), F32)
    for j in range(t_new):
        p_j = jnp.exp(s_new[j] - m)
        l = l + p_j
        o = o + p_j * vn[j:j + 1, :]
    for i in range(n_pages):
        o = o + lax.dot_general(p_past[:, i * PAGE:(i + 1) * PAGE].astype(BF16),
                                v_pages[i][...].astype(BF16), nt, preferred_element_type=F32)
    o = o / l
    for t in range(t_new):
        blk = jnp.where(head_mask, o[t * HEADS:(t + 1) * HEADS, :], 0.0)
        o_ref[t:t + 1, :] = jnp.sum(blk, axis=0, keepdims=True)


def _fox_sample(layer, page_table, q3, k3, v3, lf3t, cache_k, cache_v, cache_logf_t):
    n_seq, t_new, _ = q3.shape
    n_pages = page_table.shape[1]
    kern = functools.partial(_fox_sample_kernel, n_pages=n_pages, t_new=t_new)
    new_spec = lambda w: pl.BlockSpec((None, t_new, w), lambda b, pt: (b, 0, 0))

    def page_spec(i, rows, w):
        return pl.BlockSpec((None, None, rows, w), lambda b, pt: (layer, pt[b, i], 0, 0))

    grid_spec = pltpu.PrefetchScalarGridSpec(
        num_scalar_prefetch=1,
        grid=(n_seq,),
        in_specs=[new_spec(BRANCH_W), new_spec(BRANCH_W), new_spec(BRANCH_W),
                  pl.BlockSpec((None, HEADS, t_new), lambda b, pt: (b, 0, 0))]
                 + [page_spec(i, BRANCH_W, PAGE) for i in range(n_pages)]
                 + [page_spec(i, BRANCH_W, PAGE) for i in range(n_pages)]
                 + [page_spec(i, HEADS, PAGE) for i in range(n_pages)],
        out_specs=pl.BlockSpec((None, t_new, BRANCH_W), lambda b, pt: (b, 0, 0)),
    )
    return pl.pallas_call(
        kern,
        grid_spec=grid_spec,
        out_shape=jax.ShapeDtypeStruct((n_seq, t_new, BRANCH_W), F32),
        compiler_params=_cparams(("parallel",)),
        name="fox_sample",
    )(page_table, q3, k3, v3, lf3t, *([cache_k] * n_pages), *([cache_v] * n_pages),
      *([cache_logf_t] * n_pages))


def _outproj_kernel(*refs, gated):
    if gated:
        x_ref, oa_ref, ob_ref, oc_ref, gm_ref, pa_ref, pb_ref, pc_ref, wo_ref, gp_ref, y_ref = refs
        a, b, c = oa_ref[...], ob_ref[...], oc_ref[...]
    else:
        (x_ref, oa_ref, ob_ref, oc_ref, ga_ref, gb_ref, gc_ref, gm_ref, pa_ref, pb_ref, pc_ref, wo_ref,
         gp_ref, y_ref) = refs
        a, b, c = oa_ref[...] * ga_ref[...], ob_ref[...] * gb_ref[...], oc_ref[...] * gc_ref[...]
    ya = _dot(a, pa_ref[...])
    yb = _dot(b, pb_ref[...])
    yc = _dot(c, pc_ref[...])
    m = (gm_ref[:, 0:D_MODEL] * ya + gm_ref[:, D_MODEL:2 * D_MODEL] * yb
         + gm_ref[:, 2 * D_MODEL:3 * D_MODEL] * yc)
    y = _dot(m, wo_ref[...])
    ms = jnp.mean(y * y, axis=-1, keepdims=True)
    y_ref[...] = x_ref[...] + y * lax.rsqrt(ms + EPS) * gp_ref[...]


def _outproj(x2d, branches, gates, gm, wl, tm):
    n = x2d.shape[0]
    row = lambda i: (i, 0)
    const = lambda i: (0, 0)
    wide = pl.BlockSpec((tm, BRANCH_W), row)
    proj = pl.BlockSpec((BRANCH_W, D_MODEL), const)
    acts = list(branches) + ([] if gates is None else list(gates))
    return pl.pallas_call(
        functools.partial(_outproj_kernel, gated=gates is None),
        grid=(n // tm,),
        in_specs=[pl.BlockSpec((tm, D_MODEL), row)] + [wide] * len(acts)
                 + [pl.BlockSpec((tm, 3 * D_MODEL), row), proj, proj, proj,
                    pl.BlockSpec((D_MODEL, D_MODEL), const), pl.BlockSpec((1, D_MODEL), const)],
        out_specs=pl.BlockSpec((tm, D_MODEL), row),
        out_shape=jax.ShapeDtypeStruct((n, D_MODEL), F32),
        compiler_params=_cparams(("parallel",)),
        name="outproj",
    )(x2d, *acts, gm, wl["proj_a"], wl["proj_b"], wl["proj_c"], wl["w_out"], wl["g_post"])


def _layer_weights(l, w_cat, p):
    row = lambda a: a[l][None, :]
    zeros = jnp.zeros((LORA, BRANCH_W), BF16)
    return {
        "w_cat": w_cat[l],
        "g_pre": row(p["norm_pre"]), "g_post": row(p["norm_post"]),
        "bf_pad": jnp.pad(p["b_forget"][l], (0, LANES - HEADS))[None, :],
        "mu": row(p["rw_mu"]), "w0": row(p["rw_w0"]), "a0": row(p["rw_a0"]),
        "w2p": jnp.concatenate([p["rw_w2"][l].astype(BF16), zeros], axis=0),
        "a2p": jnp.concatenate([zeros, p["rw_a2"][l].astype(BF16)], axis=0),
        "kk": row(p["rw_kk"]), "ka": row(p["rw_ka"]), "rk": row(p["rw_rk"]),
        "lnw": row(p["rw_lnw"]), "lnb": row(p["rw_lnb"]),
        "pool_w": p["pool_w"][l].astype(BF16), "pool_scale": row(p["pool_scale"]),
        "proj_a": p["proj_a"][l].astype(BF16), "proj_b": p["proj_b"][l].astype(BF16),
        "proj_c": p["proj_c"][l].astype(BF16), "w_out": p["w_out"][l].astype(BF16),
    }


_SRC_F = RW_SHIFT + 3072
_W_SEGMENTS = (
    (0, RW_SHIFT, C_ZRW),
    (RW_SHIFT, BRANCH_W, C_GRW),
    (RW_SHIFT + 512, BRANCH_W, C_UPOOL),
    (RW_SHIFT + 1024, BRANCH_W, C_GPOOL),
    (RW_SHIFT + 1536, 3 * BRANCH_W, C_QKV),
    (_SRC_F + HEADS, BRANCH_W, C_GFOX),
    (_SRC_F + HEADS + BRANCH_W, 3 * D_MODEL, C_MERGE),
)


def _wprep_kernel(w_ref, o_ref):
    for src, width, dst in _W_SEGMENTS:
        o_ref[:, dst:dst + width] = w_ref[:, src:src + width].astype(BF16)
    tail = w_ref[:, _SRC_F:_SRC_F + LANES]
    keep = lax.broadcasted_iota(jnp.int32, tail.shape, 1) < HEADS
    o_ref[:, C_F:W_COLS] = jnp.where(keep, tail, 0.0).astype(BF16)


def _reorder_w_in(w_in, rows=128):
    depth, d_in, n_in = w_in.shape
    return pl.pallas_call(
        _wprep_kernel,
        grid=(depth, d_in // rows),
        in_specs=[pl.BlockSpec((None, rows, n_in), lambda l, i: (l, i, 0))],
        out_specs=pl.BlockSpec((None, rows, W_COLS), lambda l, i: (l, i, 0)),
        out_shape=jax.ShapeDtypeStruct((depth, d_in, W_COLS), BF16),
        compiler_params=_cparams(("parallel", "parallel")),
        name="wprep",
    )(w_in)


def _prompt_layer(x2d, wl, layer, kv_bufs, n_seq, t_seq):
    (gm, qb, kt_buf, vt_buf, kb, vb, grw, gpool, gfox, u, zrw, lf) = _inproj(
        x2d, wl["g_pre"], wl["w_cat"], wl["bf_pad"], 256, HEAD_DIM ** -0.5 * LOG2E, kv_bufs, layer)
    zeros_last = jnp.zeros((n_seq, 1, RW_SHIFT), F32)
    pre = _rwkv_pre(zrw, zeros_last, wl, 512, t_seq)
    o_rw, s_new = _rwkv_chunk(pre, grw, wl["lnw"], wl["lnb"], n_seq, t_seq, 8)
    o_pool, pool_buf = _pool_prompt(u, gpool, wl["pool_w"], wl["pool_scale"], n_seq, t_seq)
    c, ct = _fox_cum(lf, n_seq, t_seq)
    o_fox = _fox_prompt(qb, kb, vb, c, ct, gfox, n_seq, t_seq, 512)
    y = _outproj(x2d, (o_rw, o_pool, o_fox), None, gm, wl, 512)
    shift = zrw.reshape(n_seq, t_seq, RW_SHIFT)[:, -1]
    state = (s_new, shift, pool_buf, lf[:, :HEADS].reshape(n_seq, t_seq, HEADS))
    return y, state, (kt_buf, vt_buf)


def _sample_layer(x2d, wl, layer, n_seq, t_new, state_bl, st_shift, st_pool, caches, page_table):
    (gm, qb, k, v, _, _, grw, gpool, gfox, u, zrw, lf) = _inproj(
        x2d, wl["g_pre"], wl["w_cat"], wl["bf_pad"], 256, HEAD_DIM ** -0.5)
    tm = lambda a: jnp.transpose(a, (1, 0, 2))
    z3 = zrw.reshape(n_seq, t_new, RW_SHIFT)
    o_rw_tm, s_new_bl = _rwkv_sample(layer, tm(z3), st_shift, wl, state_bl)
    o_rw = tm(o_rw_tm).reshape(n_seq * t_new, BRANCH_W)
    s_new = jnp.transpose(s_new_bl, (3, 0, 1, 2))
    past = page_table.shape[1] * PAGE
    o_pool_tm, pool_buf_tm = _pool_sample(tm(st_pool), tm(u.reshape(n_seq, t_new, BRANCH_W)),
                                          wl["pool_w"], wl["pool_scale"], past)
    o_pool = tm(o_pool_tm).reshape(n_seq * t_new, BRANCH_W)
    cache_k, cache_v, cache_logf_t = caches
    tri = lambda a, w: a.reshape(n_seq, t_new, w)
    lf3 = lf[:, :HEADS].reshape(n_seq, t_new, HEADS)
    o_fox3 = _fox_sample(layer, page_table, tri(qb, BRANCH_W), tri(k, BRANCH_W), tri(v, BRANCH_W),
                         jnp.transpose(lf3, (0, 2, 1)), cache_k, cache_v, cache_logf_t)
    y = _outproj(x2d, (o_rw, o_pool, o_fox3.reshape(n_seq * t_new, BRANCH_W)), (grw, gpool, gfox), gm,
                 wl, 256)
    state = (s_new, z3[:, -1], tm(pool_buf_tm),
             k.reshape(n_seq, t_new, HEADS, HEAD_DIM), v.reshape(n_seq, t_new, HEADS, HEAD_DIM), lf3)
    return y, state


def kernel(x_prompt, x_sample, state_rwkv, state_shift, state_pool, cache_k, cache_v, cache_logf, page_table, norm_pre, norm_post, w_in, b_forget, rw_mu, rw_w0, rw_w2, rw_a0, rw_a2, rw_kk, rw_ka, rw_rk, rw_lnw, rw_lnb, pool_w, pool_scale, proj_a, proj_b, proj_c, w_out):
    depth = w_in.shape[0]
    bp, sp, _ = x_prompt.shape
    bs, ts, _ = x_sample.shape
    params = dict(norm_pre=norm_pre, norm_post=norm_post, b_forget=b_forget, rw_mu=rw_mu, rw_w0=rw_w0,
                  rw_w2=rw_w2, rw_a0=rw_a0, rw_a2=rw_a2, rw_kk=rw_kk, rw_ka=rw_ka, rw_rk=rw_rk,
                  rw_lnw=rw_lnw, rw_lnb=rw_lnb, pool_w=pool_w, pool_scale=pool_scale, proj_a=proj_a,
                  proj_b=proj_b, proj_c=proj_c, w_out=w_out)
    w_cat = _reorder_w_in(w_in)
    n_pool = cache_k.shape[1]
    keys_minor = lambda c: jnp.transpose(c, (0, 1, 3, 4, 2)).reshape(depth, n_pool, BRANCH_W, PAGE)
    caches = (keys_minor(cache_k), keys_minor(cache_v), jnp.transpose(cache_logf, (0, 1, 3, 2)))
    state_bl = jnp.transpose(state_rwkv, (0, 2, 3, 4, 1))
    hp = x_prompt.reshape(bp * sp, D_MODEL)
    hs = x_sample.reshape(bs * ts, D_MODEL)
    kv_bufs = (jnp.zeros((depth, bp, BRANCH_W, sp), F32), jnp.zeros((depth, bp, BRANCH_W, sp), F32))
    st_p, st_s = [], []
    for l in range(depth):
        wl = _layer_weights(l, w_cat, params)
        hp, sp_state, kv_bufs = _prompt_layer(hp, wl, l, kv_bufs, bp, sp)
        hs, ss_state = _sample_layer(hs, wl, l, bs, ts, state_bl, state_shift[l], state_pool[l],
                                     caches, page_table)
        st_p.append(sp_state)
        st_s.append(ss_state)
    stack = lambda states, i: jnp.stack([s[i] for s in states])
    rows_major = lambda buf: jnp.transpose(buf.reshape(depth, bp, HEADS, HEAD_DIM, sp), (0, 1, 4, 2, 3))
    return (hp.reshape(bp, sp, D_MODEL), hs.reshape(bs, ts, D_MODEL),
            stack(st_p, 0), stack(st_s, 0), stack(st_p, 1), stack(st_s, 1), stack(st_p, 2), stack(st_s, 2),
            rows_major(kv_bufs[0]), stack(st_s, 3), rows_major(kv_bufs[1]), stack(st_s, 4),
            stack(st_p, 3), stack(st_s, 5))
```

```python
import functools

import jax
import jax.numpy as jnp
from jax import lax
from jax.experimental import pallas as pl
from jax.experimental.pallas import tpu as pltpu

F32 = jnp.float32
BF16 = jnp.bfloat16
HIGHEST = lax.Precision.HIGHEST

D_MODEL = 1024
HEADS = 8
HEAD_DIM = 64
BRANCH_W = HEADS * HEAD_DIM
LORA = 64
RW_SHIFT = 3 * BRANCH_W + 2 * LORA
POOL_WINDOWS = (2, 4, 8, 16)
POOL_GW = 128
POOL_BUF = 15
PAGE = 128
CHUNK = 64
CPI = 4
LANES = 128
EPS = 1e-6
LNX_EPS = 64e-5
LOG2E = 1.4426950408889634
NEG_EXP_M_HALF = -0.6065306597126334

C_MERGE = 0
C_QKV = 3 * D_MODEL
C_GRW = C_QKV + 3 * BRANCH_W
C_GPOOL = C_GRW + BRANCH_W
C_GFOX = C_GPOOL + BRANCH_W
C_UPOOL = C_GFOX + BRANCH_W
C_ZRW = C_UPOOL + BRANCH_W
C_F = C_ZRW + RW_SHIFT
W_COLS = C_F + LANES

VMEM_LIMIT = 56 * 1024 * 1024


def _cparams(sem):
    return pltpu.CompilerParams(dimension_semantics=sem, vmem_limit_bytes=VMEM_LIMIT)


def _sigmoid(x):
    return 1.0 / (1.0 + jnp.exp(-x))


def _log_sigmoid(x):
    return jnp.minimum(x, 0.0) - jnp.log1p(jnp.exp(-jnp.abs(x)))


def _dot(a, b):
    return jnp.dot(a.astype(BF16), b.astype(BF16), preferred_element_type=F32)


def _dot_nt(a, b):
    return lax.dot_general(a.astype(BF16), b.astype(BF16), (((1,), (1,)), ((), ())),
                           preferred_element_type=F32)


def _dot_tn(a, b):
    return lax.dot_general(a.astype(BF16), b.astype(BF16), (((0,), (0,)), ((), ())),
                           preferred_element_type=F32)


def _inproj_kernel(*refs, q_scale, keys_minor):
    x_ref, g_ref, w_ref, bf_ref = refs[:4]
    outs = refs[6:] if keys_minor else refs[4:]
    (gm_ref, q_ref, k_ref, v_ref, kb_ref, vb_ref, grw_ref, gpool_ref, gfox_ref, u_ref, zrw_ref,
     lf_ref) = outs
    x = x_ref[...]
    ms = jnp.mean(x * x, axis=-1, keepdims=True)
    xn = (x * lax.rsqrt(ms + EPS) * g_ref[...]).astype(BF16)

    def mm(c0, width):
        return jnp.dot(xn, w_ref[:, c0:c0 + width], preferred_element_type=F32)

    for i in range(3):
        gm_ref[:, i * D_MODEL:(i + 1) * D_MODEL] = _sigmoid(mm(C_MERGE + i * D_MODEL, D_MODEL))
    q_ref[...] = (mm(C_QKV, BRANCH_W) * q_scale).astype(BF16)
    k = mm(C_QKV + BRANCH_W, BRANCH_W)
    k_ref[...] = k.T if keys_minor else k
    kb_ref[...] = k.astype(BF16)
    v = mm(C_QKV + 2 * BRANCH_W, BRANCH_W)
    v_ref[...] = v.T if keys_minor else v
    vb_ref[...] = v.astype(BF16)
    for ref, c0 in ((grw_ref, C_GRW), (gpool_ref, C_GPOOL), (gfox_ref, C_GFOX)):
        g = mm(c0, BRANCH_W)
        ref[...] = g * _sigmoid(g)
    u_ref[...] = mm(C_UPOOL, BRANCH_W)
    zrw_ref[:, 0:3 * BRANCH_W] = mm(C_ZRW, 3 * BRANCH_W)
    zrw_ref[:, 3 * BRANCH_W:RW_SHIFT] = mm(C_ZRW + 3 * BRANCH_W, 2 * LORA)
    f = mm(C_F, LANES) + bf_ref[...]
    head_lane = lax.broadcasted_iota(jnp.int32, f.shape, 1) < HEADS
    lf_ref[...] = jnp.where(head_lane, _log_sigmoid(f), 0.0)


def _inproj(x2d, g_pre, w_cat, bf_pad, tm, q_scale, layer, kv_bufs=None):
    n = x2d.shape[0]
    row = lambda i: (i, 0)
    const = lambda i: (0, 0)
    widths = [(3 * D_MODEL, F32), (BRANCH_W, BF16), (BRANCH_W, F32), (BRANCH_W, F32),
              (BRANCH_W, BF16), (BRANCH_W, BF16), (BRANCH_W, F32), (BRANCH_W, F32),
              (BRANCH_W, F32), (BRANCH_W, F32), (RW_SHIFT, F32), (LANES, F32)]
    in_specs = [pl.BlockSpec((tm, D_MODEL), row),
                pl.BlockSpec((1, D_MODEL), const),
                pl.BlockSpec((None, D_MODEL, W_COLS), lambda i: (layer, 0, 0), pipeline_mode=pl.Buffered(1)),
                pl.BlockSpec((1, LANES), const)]
    out_specs = [pl.BlockSpec((tm, w), row) for w, _ in widths]
    out_shape = [jax.ShapeDtypeStruct((n, w), dt) for w, dt in widths]
    args = [x2d, g_pre, w_cat, bf_pad]
    aliases = {}
    if kv_bufs is not None:
        tiles = kv_bufs[0].shape[3] // tm
        kv_spec = pl.BlockSpec((None, None, BRANCH_W, tm), lambda i: (layer, i // tiles, 0, i % tiles))
        for j, buf in enumerate(kv_bufs):
            in_specs.append(pl.BlockSpec(memory_space=pl.ANY))
            args.append(buf)
            out_specs[2 + j] = kv_spec
            out_shape[2 + j] = jax.ShapeDtypeStruct(buf.shape, buf.dtype)
            aliases[4 + j] = 2 + j
    return pl.pallas_call(
        functools.partial(_inproj_kernel, q_scale=q_scale, keys_minor=kv_bufs is not None),
        grid=(n // tm,),
        in_specs=in_specs,
        out_specs=out_specs,
        out_shape=out_shape,
        input_output_aliases=aliases,
        compiler_params=_cparams(("parallel",)),
        name="inproj",
    )(*args)


def _seg_sum(x, lo_mask):
    s_lo = jnp.sum(jnp.where(lo_mask, x, 0.0), axis=-1, keepdims=True)
    s_hi = jnp.sum(jnp.where(lo_mask, 0.0, x), axis=-1, keepdims=True)
    return jnp.where(lo_mask, s_lo, s_hi)


def _seg_sum_wide(x):
    tt = x.shape[0]
    lo_mask = lax.broadcasted_iota(jnp.int32, (tt, LANES), 1) < HEAD_DIM
    parts = [_seg_sum(x[:, p * LANES:(p + 1) * LANES], lo_mask) for p in range(x.shape[1] // LANES)]
    return jnp.concatenate(parts, axis=-1)


def _rwkv_pre_kernel(z_ref, zp_ref, zl_ref, mu_ref, w0_ref, w2_ref, a0_ref, a2_ref, kk_ref, ka_ref,
                     rk_ref, at_ref, rt_ref, bt_ref, kt_ref, vb_ref, bonus_ref, pc_ref,
                     *, tt, t_seq):
    i = pl.program_id(0)
    z = z_ref[...]
    starts_seq = (i * tt) % t_seq == 0
    prev_row = jnp.where(starts_seq, zl_ref[...], zp_ref[7:8, :])
    row = lax.broadcasted_iota(jnp.int32, (tt, 1), 0)
    z_prev = jnp.where(row == 0, prev_row, pltpu.roll(z, 1, axis=0))
    zs = z + (z_prev - z) * mu_ref[...]
    w3 = BRANCH_W
    r = zs[:, 0:w3]
    k = zs[:, w3:2 * w3]
    v = zs[:, 2 * w3:3 * w3]
    lora = zs[:, 3 * w3:RW_SHIFT]
    wx = w0_ref[...] + jnp.dot(jnp.tanh(lora).astype(BF16), w2_ref[...], preferred_element_type=F32)
    logd = _sigmoid(wx) * NEG_EXP_M_HALF
    a = _sigmoid(a0_ref[...] + jnp.dot(lora.astype(BF16), a2_ref[...], preferred_element_type=F32))
    kk = k * kk_ref[...]
    k2 = k * (1.0 + (a - 1.0) * ka_ref[...])
    kk = kk / jnp.maximum(jnp.sqrt(_seg_sum_wide(kk * kk)), 1e-12)
    bonus_ref[...] = _seg_sum_wide(r * k2 * rk_ref[...]) * v
    ri = lax.broadcasted_iota(jnp.int32, (CHUNK, CHUNK), 0)
    ci = lax.broadcasted_iota(jnp.int32, (CHUNK, CHUNK), 1)
    tri = jnp.where(ci <= ri, 1.0, 0.0).astype(F32)
    cl = jnp.concatenate(
        [jnp.dot(tri, logd[c * CHUNK:(c + 1) * CHUNK, :], preferred_element_type=F32, precision=HIGHEST)
         for c in range(tt // CHUNK)], axis=0)
    e_neg = jnp.exp(-cl)
    at_ref[...] = (-kk * jnp.exp(cl - logd)).astype(BF16)
    rt_ref[...] = (r * jnp.exp(cl)).astype(BF16)
    bt_ref[...] = (kk * a * e_neg).astype(BF16)
    kt_ref[...] = (k2 * e_neg).astype(BF16)
    vb_ref[...] = v.astype(BF16)
    for c in range(tt // CHUNK):
        last = cl[(c + 1) * CHUNK - 1:(c + 1) * CHUNK, :]
        pc_ref[c * 8:(c + 1) * 8, :] = jnp.broadcast_to(jnp.exp(last), (8, BRANCH_W))


def _rwkv_pre(z, z_last, wl, tt, t_seq):
    n = z.shape[0]
    row = lambda i: (i, 0)
    const = lambda i: (0, 0)
    vec = lambda w: pl.BlockSpec((1, w), const)
    nb8 = tt // 8
    kern = functools.partial(_rwkv_pre_kernel, tt=tt, t_seq=t_seq)
    wide = lambda dt: jax.ShapeDtypeStruct((n, BRANCH_W), dt)
    return pl.pallas_call(
        kern,
        grid=(n // tt,),
        in_specs=[pl.BlockSpec((tt, RW_SHIFT), row),
                  pl.BlockSpec((8, RW_SHIFT), lambda i: (jnp.maximum(i * nb8 - 1, 0), 0)),
                  pl.BlockSpec((None, 1, RW_SHIFT), lambda i: ((i * tt) // t_seq, 0, 0)),
                  vec(RW_SHIFT), vec(BRANCH_W),
                  pl.BlockSpec((LANES, BRANCH_W), const),
                  vec(BRANCH_W),
                  pl.BlockSpec((LANES, BRANCH_W), const),
                  vec(BRANCH_W), vec(BRANCH_W), vec(BRANCH_W)],
        out_specs=[pl.BlockSpec((tt, BRANCH_W), row)] * 6
                  + [pl.BlockSpec((tt // CHUNK * 8, BRANCH_W), row)],
        out_shape=[wide(BF16), wide(BF16), wide(BF16), wide(BF16), wide(BF16), wide(F32),
                   jax.ShapeDtypeStruct((n // CHUNK * 8, BRANCH_W), F32)],
        compiler_params=_cparams(("parallel",)),
        name="rwkv_pre",
    )(z, z, z_last, wl["mu"], wl["w0"], wl["w2p"], wl["a0"], wl["a2p"], wl["kk"], wl["ka"], wl["rk"])


def _rwkv_chunk_kernel(at_ref, rt_ref, bt_ref, kt_ref, vb_ref, bonus_ref, pc_ref, gate_ref, lnw_ref,
                       lnb_ref, o_ref, sout_ref, s_scr, *, cpb):
    g = pl.program_id(1)
    c2 = 2 * CHUNK
    pairs = HEADS // 2

    @pl.when(g == 0)
    def _():
        s_scr[...] = jnp.zeros(s_scr.shape, F32)

    lane = lax.broadcasted_iota(jnp.int32, (CHUNK, LANES), 1)
    lo = lane < HEAD_DIM
    ri = lax.broadcasted_iota(jnp.int32, (c2, c2), 0)
    ci = lax.broadcasted_iota(jnp.int32, (c2, c2), 1)
    same = (ri // CHUNK) == (ci // CHUNK)
    strict = same & (ci < ri)
    incl = same & (ci <= ri)
    eye = jnp.where(ri == ci, 1.0, 0.0).astype(F32)

    def stack2(x):
        x = x.astype(F32)
        return jnp.concatenate([jnp.where(lo, x, 0.0), jnp.where(lo, 0.0, x)], axis=0)

    def body(i, carry):
        rows = [pl.ds(pl.multiple_of((i * CPI + cc) * CHUNK, CHUNK), CHUNK) for cc in range(CPI)]
        pc_row = [pl.ds(pl.multiple_of((i * CPI + cc) * 8, 8), 1) for cc in range(CPI)]
        lanes = [slice(p * LANES, (p + 1) * LANES) for p in range(pairs)]
        units = [(rows[cc], lanes[p]) for cc in range(CPI) for p in range(pairs)]
        each = lambda f, *xs: [f(*args) for args in zip(*xs)]
        a_s = [stack2(at_ref[rw, ln]) for rw, ln in units]
        r_s = [stack2(rt_ref[rw, ln]) for rw, ln in units]
        b_s = [stack2(bt_ref[rw, ln]) for rw, ln in units]
        k_s = [stack2(kt_ref[rw, ln]) for rw, ln in units]
        vp = [vb_ref[rw, ln].astype(F32) for rw, ln in units]
        v_s = [jnp.concatenate([x[:, :HEAD_DIM], x[:, HEAD_DIM:]], axis=0) for x in vp]
        pc = [pc_ref[pc_row[cc], lanes[p]] for cc in range(CPI) for p in range(pairs)]
        nmat = each(lambda a, b: jnp.where(strict, _dot_nt(a, b), 0.0), a_s, b_s)
        lak = each(lambda a, k: jnp.where(strict, _dot_nt(a, k), 0.0), a_s, k_s)
        mrb = each(lambda r, b: jnp.where(incl, _dot_nt(r, b), 0.0), r_s, b_s)
        mrk = each(lambda r, k: jnp.where(incl, _dot_nt(r, k), 0.0), r_s, k_s)
        lakv = each(_dot, lak, v_s)
        mrkv = each(_dot, mrk, v_s)
        vtk = each(_dot_tn, v_s, k_s)
        tinv = [eye + n for n in nmat]
        pw = nmat
        for _ in range(5):
            pw = each(_dot, pw, pw)
            tinv = each(lambda t, q: t + _dot(t, q), tinv, pw)
        a_hat = each(_dot, tinv, a_s)
        u_hat = each(_dot, tinv, lakv)
        r_hat = each(lambda r, m, a: r + _dot(m, a), r_s, mrb, a_hat)
        y_hat = each(lambda m, u, y: _dot(m, u) + y, mrb, u_hat, mrkv)
        gmat = each(lambda a, b, x: (eye + _dot_tn(a, b)) * x, a_hat, b_s, pc)
        hmat = each(lambda u, b, y, x: (_dot_tn(u, b) + y) * x, u_hat, b_s, vtk, pc)
        s = [s_scr[p] for p in range(pairs)]
        for cc in range(CPI):
            sel = slice(cc * pairs, (cc + 1) * pairs)
            y_s = each(lambda r, st, y: _dot_nt(r, st) + y, r_hat[sel], s, y_hat[sel])
            s = each(lambda st, gm, hm: _dot(st, gm) + hm, s, gmat[sel], hmat[sel])
            for p in range(pairs):
                y = jnp.concatenate([y_s[p][:CHUNK], y_s[p][CHUNK:]], axis=1)
                mean = _seg_sum(y, lo) * (1.0 / HEAD_DIM)
                yc = y - mean
                var = _seg_sum(yc * yc, lo) * (1.0 / HEAD_DIM)
                ln = lanes[p]
                o_rw = yc * lax.rsqrt(var + LNX_EPS) * lnw_ref[:, ln] + lnb_ref[:, ln] + bonus_ref[rows[cc], ln]
                o_ref[rows[cc], ln] = (o_rw * gate_ref[rows[cc], ln]).astype(BF16)
        for p in range(pairs):
            s_scr[p] = s[p]
        return carry

    lax.fori_loop(0, cpb // CPI, body, 0)

    @pl.when(g == pl.num_programs(1) - 1)
    def _():
        for p in range(pairs):
            s = s_scr[p]
            sout_ref[2 * p] = s[:, :HEAD_DIM]
            sout_ref[2 * p + 1] = s[:, HEAD_DIM:]


def _rwkv_chunk(pre, gate, lnw, lnb, n_seq, t_seq, cpb):
    at, rt, bt, kt, vb, bonus, pc = pre
    n = at.shape[0]
    ng = t_seq // (cpb * CHUNK)
    rows = cpb * CHUNK
    tile = lambda b, g: (b * ng + g, 0)
    const = lambda b, g: (0, 0)
    state = pl.BlockSpec((None, HEADS, HEAD_DIM, HEAD_DIM), lambda b, g: (b, 0, 0, 0))
    kern = functools.partial(_rwkv_chunk_kernel, cpb=cpb)
    return pl.pallas_call(
        kern,
        grid=(n_seq, ng),
        in_specs=[pl.BlockSpec((rows, BRANCH_W), tile)] * 6
                 + [pl.BlockSpec((cpb * 8, BRANCH_W), tile),
                    pl.BlockSpec((rows, BRANCH_W), tile),
                    pl.BlockSpec((1, BRANCH_W), const),
                    pl.BlockSpec((1, BRANCH_W), const)],
        out_specs=[pl.BlockSpec((rows, BRANCH_W), tile), state],
        out_shape=[jax.ShapeDtypeStruct((n, BRANCH_W), BF16),
                   jax.ShapeDtypeStruct((n_seq, HEADS, HEAD_DIM, HEAD_DIM), F32)],
        scratch_shapes=[pltpu.VMEM((HEADS // 2, HEAD_DIM, LANES), F32)],
        compiler_params=_cparams(("parallel", "arbitrary")),
        name="rwkv_chunk",
    )(at, rt, bt, kt, vb, bonus, pc, gate, lnw, lnb)


def _rwkv_sample_kernel(zr_ref, zk_ref, zv_ref, zl_ref, lr_ref, lk_ref, lv_ref, ll_ref,
                        mur_ref, muk_ref, muv_ref, mul_ref, w0_ref, w2_ref, a0_ref, a2_ref, kk_ref,
                        ka_ref, rk_ref, lnw_ref, lnb_ref, s_ref, o_ref, sout_ref, tr_ref, yt_ref,
                        *, t_new):
    n_b = zr_ref.shape[1]
    lo = lax.broadcasted_iota(jnp.int32, (n_b, LANES), 1) < HEAD_DIM

    def shifted(cur_ref, last_ref, mu_ref, t):
        cur = cur_ref[t]
        prev = last_ref[...] if t == 0 else cur_ref[t - 1]
        return cur + (prev - cur) * mu_ref[...]

    for t in range(t_new):
        r = shifted(zr_ref, lr_ref, mur_ref, t)
        k = shifted(zk_ref, lk_ref, muk_ref, t)
        v = shifted(zv_ref, lv_ref, muv_ref, t)
        lora = shifted(zl_ref, ll_ref, mul_ref, t)
        wx = w0_ref[...] + jnp.dot(jnp.tanh(lora).astype(BF16), w2_ref[...], preferred_element_type=F32)
        logd = _sigmoid(wx) * NEG_EXP_M_HALF
        a = _sigmoid(a0_ref[...] + jnp.dot(lora.astype(BF16), a2_ref[...], preferred_element_type=F32))
        kk = k * kk_ref[...]
        k2 = k * (1.0 + (a - 1.0) * ka_ref[...])
        kk = kk / jnp.maximum(jnp.sqrt(_seg_sum(kk * kk, lo)), 1e-12)
        o_ref[t] = _seg_sum(r * k2 * rk_ref[...], lo) * v
        for i, x in enumerate((-kk, jnp.exp(logd), kk * a, k2, r, v)):
            tr_ref[t, i] = x.T

    for hh in range(2):
        h0 = hh * HEAD_DIM

        def row_step(vi, carry):
            s_v = s_ref[hh, vi]
            for t in range(t_new):
                nkk = tr_ref[t, 0, h0:h0 + HEAD_DIM, :]
                dec = tr_ref[t, 1, h0:h0 + HEAD_DIM, :]
                kka = tr_ref[t, 2, h0:h0 + HEAD_DIM, :]
                k2 = tr_ref[t, 3, h0:h0 + HEAD_DIM, :]
                r = tr_ref[t, 4, h0:h0 + HEAD_DIM, :]
                vv = tr_ref[t, 5, pl.ds(h0 + vi, 1), :]
                sa = jnp.sum(s_v * nkk, axis=0, keepdims=True)
                s_v = s_v * dec + sa * kka + vv * k2
                yt_ref[t, pl.ds(h0 + vi, 1), :] = jnp.sum(s_v * r, axis=0, keepdims=True)
            sout_ref[hh, vi] = s_v
            return carry

        lax.fori_loop(0, HEAD_DIM, row_step, 0, unroll=4)

    for t in range(t_new):
        y = yt_ref[t].T
        mean = _seg_sum(y, lo) * (1.0 / HEAD_DIM)
        yc = y - mean
        var = _seg_sum(yc * yc, lo) * (1.0 / HEAD_DIM)
        o_ref[t] = o_ref[t] + yc * lax.rsqrt(var + LNX_EPS) * lnw_ref[...] + lnb_ref[...]


def _rwkv_sample(layer, zt, z_last, wl, state_bl):
    t_new, n_b, _ = zt.shape
    pairs = HEADS // 2
    lora_tile = 3 * pairs
    seg = lambda s: pl.BlockSpec((t_new, n_b, LANES), lambda p, s=s: (0, 0, s * pairs + p))
    seg_last = lambda s: pl.BlockSpec((n_b, LANES), lambda p, s=s: (0, s * pairs + p))
    seg_mu = lambda s: pl.BlockSpec((1, LANES), lambda p, s=s: (0, s * pairs + p))
    vec = pl.BlockSpec((1, LANES), lambda p: (0, p))
    lora_w = pl.BlockSpec((LANES, LANES), lambda p: (0, p))
    kern = functools.partial(_rwkv_sample_kernel, t_new=t_new)
    return pl.pallas_call(
        kern,
        grid=(pairs,),
        in_specs=[seg(0), seg(1), seg(2),
                  pl.BlockSpec((t_new, n_b, LANES), lambda p: (0, 0, lora_tile)),
                  seg_last(0), seg_last(1), seg_last(2),
                  pl.BlockSpec((n_b, LANES), lambda p: (0, lora_tile)),
                  seg_mu(0), seg_mu(1), seg_mu(2),
                  pl.BlockSpec((1, LANES), lambda p: (0, lora_tile)),
                  vec, lora_w, vec, lora_w, vec, vec, vec, vec, vec,
                  pl.BlockSpec((None, 2, HEAD_DIM, HEAD_DIM, n_b), lambda p: (layer, p, 0, 0, 0))],
        out_specs=[pl.BlockSpec((t_new, n_b, LANES), lambda p: (0, 0, p)),
                   pl.BlockSpec((2, HEAD_DIM, HEAD_DIM, n_b), lambda p: (p, 0, 0, 0))],
        out_shape=[jax.ShapeDtypeStruct((t_new, n_b, BRANCH_W), F32),
                   jax.ShapeDtypeStruct((HEADS, HEAD_DIM, HEAD_DIM, n_b), F32)],
        scratch_shapes=[pltpu.VMEM((t_new, 6, LANES, n_b), F32), pltpu.VMEM((t_new, LANES, n_b), F32)],
        compiler_params=_cparams(("parallel",)),
        name="rwkv_sample",
    )(zt, zt, zt, zt, z_last, z_last, z_last, z_last, wl["mu"], wl["mu"], wl["mu"], wl["mu"],
      wl["w0"], wl["w2p"], wl["a0"], wl["a2p"], wl["kk"], wl["ka"], wl["rk"], wl["lnw"], wl["lnb"],
      state_bl)


def _pool_prompt_kernel(u_ref, pw_ref, ps_ref, gate_ref, o_ref, buf_ref, ext_ref, *, t_seq, tile):
    pad = POOL_BUF + 1
    ext_ref[0:pad, :] = jnp.zeros((pad, BRANCH_W), F32)
    ext_ref[pad:pad + t_seq, :] = u_ref[...]
    buf_ref[...] = u_ref[t_seq - POOL_BUF:t_seq, :]
    for t0 in range(0, t_seq, tile):
        pos = t0 + lax.broadcasted_iota(jnp.int32, (tile, 1), 0)
        for gi, w in enumerate(POOL_WINDOWS):
            sl = slice(gi * POOL_GW, (gi + 1) * POOL_GW)
            wsum = ext_ref[pad + t0:pad + t0 + tile, sl]
            for j in range(1, w):
                wsum = wsum + ext_ref[pad + t0 - j:pad + t0 - j + tile, sl]
            cnt = jnp.minimum(pos + 1, w).astype(F32)
            d = wsum / cnt - ext_ref[pad + t0:pad + t0 + tile, sl]
            o_pool = _dot(d, pw_ref[gi]) * ps_ref[:, sl]
            o_ref[t0:t0 + tile, sl] = (o_pool * gate_ref[t0:t0 + tile, sl]).astype(BF16)


def _pool_prompt(u, gate, pool_w, pool_scale, n_seq, t_seq):
    kern = functools.partial(_pool_prompt_kernel, t_seq=t_seq, tile=256)
    seq = pl.BlockSpec((t_seq, BRANCH_W), lambda b: (b, 0))
    return pl.pallas_call(
        kern,
        grid=(n_seq,),
        in_specs=[seq,
                  pl.BlockSpec((len(POOL_WINDOWS), POOL_GW, POOL_GW), lambda b: (0, 0, 0)),
                  pl.BlockSpec((1, BRANCH_W), lambda b: (0, 0)),
                  seq],
        out_specs=[seq, pl.BlockSpec((None, POOL_BUF, BRANCH_W), lambda b: (b, 0, 0))],
        out_shape=[jax.ShapeDtypeStruct((n_seq * t_seq, BRANCH_W), BF16),
                   jax.ShapeDtypeStruct((n_seq, POOL_BUF, BRANCH_W), F32)],
        scratch_shapes=[pltpu.VMEM((POOL_BUF + 1 + t_seq, BRANCH_W), F32)],
        compiler_params=_cparams(("parallel",)),
        name="pool_prompt",
    )(u, pool_w, pool_scale, gate)


def _pool_sample_kernel(pre_ref, u_ref, pw_ref, ps_ref, o_ref, buf_ref, *, t_new, pos0):
    def ext_row(i, sl):
        if i < POOL_BUF:
            return pre_ref[i, :, sl]
        return u_ref[i - POOL_BUF, :, sl]

    full = slice(0, BRANCH_W)
    for i in range(POOL_BUF):
        buf_ref[i] = ext_row(i + t_new, full)
    for t in range(t_new):
        for gi, w in enumerate(POOL_WINDOWS):
            sl = slice(gi * POOL_GW, (gi + 1) * POOL_GW)
            cur = ext_row(POOL_BUF + t, sl)
            wsum = cur
            for j in range(1, w):
                wsum = wsum + ext_row(POOL_BUF + t - j, sl)
            cnt = float(min(pos0 + t + 1, w))
            d = wsum / cnt - cur
            o_ref[t, :, sl] = _dot(d, pw_ref[gi]) * ps_ref[:, sl]


def _pool_sample(prefix, u3, pool_w, pool_scale, pos0):
    t_new = u3.shape[0]
    kern = functools.partial(_pool_sample_kernel, t_new=t_new, pos0=pos0)
    full3 = lambda shape: pl.BlockSpec(shape, lambda i: (0, 0, 0))
    return pl.pallas_call(
        kern,
        grid=(1,),
        in_specs=[full3(prefix.shape), full3(u3.shape), full3(pool_w.shape),
                  pl.BlockSpec((1, BRANCH_W), lambda i: (0, 0))],
        out_specs=[full3(u3.shape), full3(prefix.shape)],
        out_shape=[jax.ShapeDtypeStruct(u3.shape, F32), jax.ShapeDtypeStruct(prefix.shape, F32)],
        compiler_params=_cparams(("arbitrary",)),
        name="pool_sample",
    )(prefix, u3, pool_w, pool_scale)


def _fox_cum_kernel(lf_ref, c_ref, ct_ref, *, t_seq, tile, n_sb):
    ri = lax.broadcasted_iota(jnp.int32, (tile, tile), 0)
    ci = lax.broadcasted_iota(jnp.int32, (tile, tile), 1)
    tri = jnp.where(ci <= ri, 1.0, 0.0).astype(F32)
    e8 = jnp.where(lax.broadcasted_iota(jnp.int32, (HEADS, LANES), 0)
                   == lax.broadcasted_iota(jnp.int32, (HEADS, LANES), 1), 1.0, 0.0).astype(F32)
    carry = [jnp.zeros((1, LANES), F32)] * n_sb
    for t0 in range(0, t_seq, tile):
        rows = [slice(b * t_seq + t0, b * t_seq + t0 + tile) for b in range(n_sb)]
        c = [jnp.dot(tri, lf_ref[rows[b], :], preferred_element_type=F32, precision=HIGHEST) + carry[b]
             for b in range(n_sb)]
        ct = [lax.dot_general(e8, c[b], (((1,), (1,)), ((), ())), preferred_element_type=F32,
                              precision=HIGHEST) for b in range(n_sb)]
        for b in range(n_sb):
            c_ref[rows[b], :] = c[b]
            ct_ref[b, :, t0:t0 + tile] = ct[b]
        carry = [x[tile - 1:tile, :] for x in c]


def _fox_cum(lf, n_seq, t_seq):
    n_sb = min(n_seq, 8)
    kern = functools.partial(_fox_cum_kernel, t_seq=t_seq, tile=256, n_sb=n_sb)
    return pl.pallas_call(
        kern,
        grid=(n_seq // n_sb,),
        in_specs=[pl.BlockSpec((n_sb * t_seq, LANES), lambda b: (b, 0))],
        out_specs=[pl.BlockSpec((n_sb * t_seq, LANES), lambda b: (b, 0)),
                   pl.BlockSpec((n_sb, HEADS, t_seq), lambda b: (b, 0, 0))],
        out_shape=[jax.ShapeDtypeStruct((n_seq * t_seq, LANES), F32),
                   jax.ShapeDtypeStruct((n_seq, HEADS, t_seq), F32)],
        compiler_params=_cparams(("parallel",)),
        name="fox_cum",
    )(lf)


def _fox_prompt_kernel(q_ref, k_ref, v_ref, c_ref, ct_ref, gate_ref, o_ref, *, tq):
    p = pl.program_id(1)
    qi = pl.program_id(2)
    q = q_ref[...].astype(F32)
    lane = lax.broadcasted_iota(jnp.int32, (tq, LANES), 1)
    lo = lane < HEAD_DIM
    ri = lax.broadcasted_iota(jnp.int32, (tq, tq), 0)
    ci = lax.broadcasted_iota(jnp.int32, (tq, tq), 1)
    causal = ci <= ri
    c_all = c_ref[...]
    qh = [jnp.where(lo, q, 0.0).astype(BF16), jnp.where(lo, 0.0, q).astype(BF16)]
    cq = [jnp.sum(jnp.where(lane == 2 * p + hh, c_all, 0.0), axis=-1, keepdims=True) * LOG2E
          for hh in range(2)]

    def block(j, carry, masked):
        cols = pl.ds(pl.multiple_of(j * tq, tq), tq)
        k = k_ref[cols, :]
        v = v_ref[cols, :]
        heads = range(2)
        ck = [ct_ref[pl.ds(2 * p + hh, 1), cols] * LOG2E for hh in heads]
        s = [lax.dot_general(qh[hh], k, (((1,), (1,)), ((), ())), preferred_element_type=F32) - ck[hh]
             for hh in heads]
        if masked:
            s = [jnp.where(causal, x, -jnp.inf) for x in s]
        m_new = [jnp.maximum(carry[hh][0], jnp.max(s[hh], axis=-1, keepdims=True) + cq[hh]) for hh in heads]
        pr = [jnp.exp2(s[hh] - (m_new[hh] - cq[hh])) for hh in heads]
        alpha = [jnp.exp2(carry[hh][0] - m_new[hh]) for hh in heads]
        l = [alpha[hh] * carry[hh][1] + jnp.sum(pr[hh], axis=-1, keepdims=True) for hh in heads]
        acc = [alpha[hh] * carry[hh][2] + jnp.dot(pr[hh].astype(BF16), v, preferred_element_type=F32)
               for hh in heads]
        return tuple((m_new[hh], l[hh], acc[hh]) for hh in heads)

    one = (jnp.full((tq, 1), -jnp.inf, F32), jnp.zeros((tq, 1), F32), jnp.zeros((tq, LANES), F32))
    carry = lax.fori_loop(0, qi, lambda j, c: block(j, c, False), (one, one))
    (_, l0, acc0), (_, l1, acc1) = block(qi, carry, True)
    o_ref[...] = (jnp.where(lo, acc0 / l0, acc1 / l1) * gate_ref[...]).astype(BF16)


def _fox_prompt(qb, kb, vb, c, ct, gate, n_seq, t_seq, tq):
    nq = t_seq // tq
    pairs = HEADS // 2
    kern = functools.partial(_fox_prompt_kernel, tq=tq)
    return pl.pallas_call(
        kern,
        grid=(n_seq, pairs, nq),
        in_specs=[pl.BlockSpec((tq, LANES), lambda b, p, i: (b * nq + i, p)),
                  pl.BlockSpec((t_seq, LANES), lambda b, p, i: (b, p)),
                  pl.BlockSpec((t_seq, LANES), lambda b, p, i: (b, p)),
                  pl.BlockSpec((tq, LANES), lambda b, p, i: (b * nq + i, 0)),
                  pl.BlockSpec((None, HEADS, t_seq), lambda b, p, i: (b, 0, 0)),
                  pl.BlockSpec((tq, LANES), lambda b, p, i: (b * nq + i, p))],
        out_specs=pl.BlockSpec((tq, LANES), lambda b, p, i: (b * nq + i, p)),
        out_shape=jax.ShapeDtypeStruct((n_seq * t_seq, BRANCH_W), BF16),
        compiler_params=_cparams(("parallel", "parallel", "arbitrary")),
        name="fox_prompt",
    )(qb, kb, vb, c, ct, gate)


def _fox_sample_kernel(pt_ref, q_ref, kn_ref, vn_ref, lfn_ref, *refs, n_pages, t_new):
    del pt_ref
    k_pages = refs[0:n_pages]
    v_pages = refs[n_pages:2 * n_pages]
    lf_pages = refs[2 * n_pages:3 * n_pages]
    o_ref = refs[3 * n_pages]
    past = n_pages * PAGE
    rows = t_new * HEADS
    nt = (((1,), (1,)), ((), ()))
    lf_t = jnp.concatenate([lf_pages[i][...] for i in range(n_pages)], axis=1)
    pos = lax.broadcasted_iota(jnp.int32, (HEADS, past), 1)
    x = lf_t
    sh = 1
    while sh < past:
        x = x + jnp.where(pos + sh < past, pltpu.roll(x, past - sh, axis=1), 0.0)
        sh *= 2
    suffix = x - lf_t
    lfn = lfn_ref[...]
    cn = [lfn[:, 0:1]]
    for t in range(1, t_new):
        cn.append(cn[-1] + lfn[:, t:t + 1])
    q = q_ref[...].astype(F32)
    head_of_lane = lax.broadcasted_iota(jnp.int32, (HEADS, BRANCH_W), 1) // HEAD_DIM
    head_mask = head_of_lane == lax.broadcasted_iota(jnp.int32, (HEADS, BRANCH_W), 0)
    qbd = jnp.concatenate(
        [jnp.where(head_mask, jnp.broadcast_to(q[t:t + 1, :], (HEADS, BRANCH_W)), 0.0)
         for t in range(t_new)], axis=0)
    qbd_bf = qbd.astype(BF16)
    s_past = jnp.concatenate(
        [jnp.dot(qbd_bf, k_pages[i][...].astype(BF16), preferred_element_type=F32)
         for i in range(n_pages)], axis=1)
    s_past = s_past + jnp.concatenate([suffix + cn[t] for t in range(t_new)], axis=0)
    row_t = lax.broadcasted_iota(jnp.int32, (rows, 1), 0) // HEADS
    kn = kn_ref[...]
    vn = vn_ref[...]
    s_new = []
    for j in range(t_new):
        dots = jnp.sum(qbd * kn[j:j + 1, :], axis=-1, keepdims=True)
        bias = jnp.concatenate([cn[t] - cn[j] for t in range(t_new)], axis=0)
        s_new.append(jnp.where(row_t >= j, dots + bias, -jnp.inf))
    m = jnp.max(s_past, axis=-1, keepdims=True)
    for j in range(t_new):
        m = jnp.maximum(m, s_new[j])
    p_past = jnp.exp(s_past - m)
    l = jnp.sum(p_past, axis=-1, keepdims=True)
    o = jnp.zeros((rows, BRANCH_W), F32)
    for j in range(t_new):
        p_j = jnp.exp(s_new[j] - m)
        l = l + p_j
        o = o + p_j * vn[j:j + 1, :]
    for i in range(n_pages):
        o = o + lax.dot_general(p_past[:, i * PAGE:(i + 1) * PAGE].astype(BF16),
                                v_pages[i][...].astype(BF16), nt, preferred_element_type=F32)
    o = o / l
    for t in range(t_new):
        blk = jnp.where(head_mask, o[t * HEADS:(t + 1) * HEADS, :], 0.0)
        o_ref[t:t + 1, :] = jnp.sum(blk, axis=0, keepdims=True)


def _fox_sample(layer, page_table, q3, k3, v3, lf3t, cache_k, cache_v, cache_logf_t):
    n_seq, t_new, _ = q3.shape
    n_pages = page_table.shape[1]
    kern = functools.partial(_fox_sample_kernel, n_pages=n_pages, t_new=t_new)
    new_spec = lambda w: pl.BlockSpec((None, t_new, w), lambda b, pt: (b, 0, 0))

    def page_spec(i, rows, w):
        return pl.BlockSpec((None, None, rows, w), lambda b, pt: (layer, pt[b, i], 0, 0))

    grid_spec = pltpu.PrefetchScalarGridSpec(
        num_scalar_prefetch=1,
        grid=(n_seq,),
        in_specs=[new_spec(BRANCH_W), new_spec(BRANCH_W), new_spec(BRANCH_W),
                  pl.BlockSpec((None, HEADS, t_new), lambda b, pt: (b, 0, 0))]
                 + [page_spec(i, BRANCH_W, PAGE) for i in range(n_pages)]
                 + [page_spec(i, BRANCH_W, PAGE) for i in range(n_pages)]
                 + [page_spec(i, HEADS, PAGE) for i in range(n_pages)],
        out_specs=pl.BlockSpec((None, t_new, BRANCH_W), lambda b, pt: (b, 0, 0)),
    )
    return pl.pallas_call(
        kern,
        grid_spec=grid_spec,
        out_shape=jax.ShapeDtypeStruct((n_seq, t_new, BRANCH_W), F32),
        compiler_params=_cparams(("parallel",)),
        name="fox_sample",
    )(page_table, q3, k3, v3, lf3t, *([cache_k] * n_pages), *([cache_v] * n_pages),
      *([cache_logf_t] * n_pages))


def _outproj_kernel(*refs, gated):
    if gated:
        x_ref, oa_ref, ob_ref, oc_ref, gm_ref, pa_ref, pb_ref, pc_ref, wo_ref, gp_ref, y_ref = refs
        a, b, c = oa_ref[...], ob_ref[...], oc_ref[...]
    else:
        (x_ref, oa_ref, ob_ref, oc_ref, ga_ref, gb_ref, gc_ref, gm_ref, pa_ref, pb_ref, pc_ref, wo_ref,
         gp_ref, y_ref) = refs
        a, b, c = oa_ref[...] * ga_ref[...], ob_ref[...] * gb_ref[...], oc_ref[...] * gc_ref[...]
    ya = _dot(a, pa_ref[...])
    yb = _dot(b, pb_ref[...])
    yc = _dot(c, pc_ref[...])
    m = (gm_ref[:, 0:D_MODEL] * ya + gm_ref[:, D_MODEL:2 * D_MODEL] * yb
         + gm_ref[:, 2 * D_MODEL:3 * D_MODEL] * yc)
    y = _dot(m, wo_ref[...])
    ms = jnp.mean(y * y, axis=-1, keepdims=True)
    y_ref[...] = x_ref[...] + y * lax.rsqrt(ms + EPS) * gp_ref[...]


def _outproj(x2d, branches, gates, gm, wl, tm):
    n = x2d.shape[0]
    row = lambda i: (i, 0)
    const = lambda i: (0, 0)
    wide = pl.BlockSpec((tm, BRANCH_W), row)
    proj = pl.BlockSpec((BRANCH_W, D_MODEL), const)
    acts = list(branches) + ([] if gates is None else list(gates))
    return pl.pallas_call(
        functools.partial(_outproj_kernel, gated=gates is None),
        grid=(n // tm,),
        in_specs=[pl.BlockSpec((tm, D_MODEL), row)] + [wide] * len(acts)
                 + [pl.BlockSpec((tm, 3 * D_MODEL), row), proj, proj, proj,
                    pl.BlockSpec((D_MODEL, D_MODEL), const), pl.BlockSpec((1, D_MODEL), const)],
        out_specs=pl.BlockSpec((tm, D_MODEL), row),
        out_shape=jax.ShapeDtypeStruct((n, D_MODEL), F32),
        compiler_params=_cparams(("parallel",)),
        name="outproj",
    )(x2d, *acts, gm, wl["proj_a"], wl["proj_b"], wl["proj_c"], wl["w_out"], wl["g_post"])


def _layer_weights(l, w_cat, p):
    row = lambda a: a[l][None, :]
    zeros = jnp.zeros((LORA, BRANCH_W), BF16)
    return {
        "w_cat": w_cat,
        "g_pre": row(p["norm_pre"]), "g_post": row(p["norm_post"]),
        "bf_pad": jnp.pad(p["b_forget"][l], (0, LANES - HEADS))[None, :],
        "mu": row(p["rw_mu"]), "w0": row(p["rw_w0"]), "a0": row(p["rw_a0"]),
        "w2p": jnp.concatenate([p["rw_w2"][l].astype(BF16), zeros], axis=0),
        "a2p": jnp.concatenate([zeros, p["rw_a2"][l].astype(BF16)], axis=0),
        "kk": row(p["rw_kk"]), "ka": row(p["rw_ka"]), "rk": row(p["rw_rk"]),
        "lnw": row(p["rw_lnw"]), "lnb": row(p["rw_lnb"]),
        "pool_w": p["pool_w"][l].astype(BF16), "pool_scale": row(p["pool_scale"]),
        "proj_a": p["proj_a"][l].astype(BF16), "proj_b": p["proj_b"][l].astype(BF16),
        "proj_c": p["proj_c"][l].astype(BF16), "w_out": p["w_out"][l].astype(BF16),
    }


_SRC_F = RW_SHIFT + 3072
_W_SEGMENTS = (
    (0, RW_SHIFT, C_ZRW),
    (RW_SHIFT, BRANCH_W, C_GRW),
    (RW_SHIFT + 512, BRANCH_W, C_UPOOL),
    (RW_SHIFT + 1024, BRANCH_W, C_GPOOL),
    (RW_SHIFT + 1536, 3 * BRANCH_W, C_QKV),
    (_SRC_F + HEADS, BRANCH_W, C_GFOX),
    (_SRC_F + HEADS + BRANCH_W, 3 * D_MODEL, C_MERGE),
)


def _wprep_kernel(wt_ref, o_ref):
    for src, width, dst in _W_SEGMENTS:
        o_ref[:, dst:dst + width] = wt_ref[src:src + width, :].T.astype(BF16)
    tail = wt_ref[_SRC_F:_SRC_F + LANES, :].T
    keep = lax.broadcasted_iota(jnp.int32, tail.shape, 1) < HEADS
    o_ref[:, C_F:W_COLS] = jnp.where(keep, tail, 0.0).astype(BF16)


def _reorder_w_in(w_in, rows=LANES):
    depth, d_in, n_in = w_in.shape
    return pl.pallas_call(
        _wprep_kernel,
        grid=(depth, d_in // rows),
        in_specs=[pl.BlockSpec((None, n_in, rows), lambda l, i: (l, 0, i))],
        out_specs=pl.BlockSpec((None, rows, W_COLS), lambda l, i: (l, i, 0)),
        out_shape=jax.ShapeDtypeStruct((depth, d_in, W_COLS), BF16),
        compiler_params=_cparams(("parallel", "parallel")),
        name="wprep",
    )(jnp.transpose(w_in, (0, 2, 1)))


def _prompt_layer(x2d, wl, layer, kv_bufs, n_seq, t_seq):
    (gm, qb, kt_buf, vt_buf, kb, vb, grw, gpool, gfox, u, zrw, lf) = _inproj(
        x2d, wl["g_pre"], wl["w_cat"], wl["bf_pad"], 256, HEAD_DIM ** -0.5 * LOG2E, layer, kv_bufs)
    zeros_last = jnp.zeros((n_seq, 1, RW_SHIFT), F32)
    pre = _rwkv_pre(zrw, zeros_last, wl, 512, t_seq)
    o_rw, s_new = _rwkv_chunk(pre, grw, wl["lnw"], wl["lnb"], n_seq, t_seq, 8)
    o_pool, pool_buf = _pool_prompt(u, gpool, wl["pool_w"], wl["pool_scale"], n_seq, t_seq)
    c, ct = _fox_cum(lf, n_seq, t_seq)
    o_fox = _fox_prompt(qb, kb, vb, c, ct, gfox, n_seq, t_seq, 512)
    y = _outproj(x2d, (o_rw, o_pool, o_fox), None, gm, wl, 512)
    shift = zrw.reshape(n_seq, t_seq, RW_SHIFT)[:, -1]
    state = (s_new, shift, pool_buf, lf[:, :HEADS].reshape(n_seq, t_seq, HEADS))
    return y, state, (kt_buf, vt_buf)


def _sample_layer(x2d, wl, layer, n_seq, t_new, state_bl, st_shift, st_pool, caches, page_table):
    (gm, qb, k, v, _, _, grw, gpool, gfox, u, zrw, lf) = _inproj(
        x2d, wl["g_pre"], wl["w_cat"], wl["bf_pad"], 256, HEAD_DIM ** -0.5, layer)
    tm = lambda a: jnp.transpose(a, (1, 0, 2))
    z3 = zrw.reshape(n_seq, t_new, RW_SHIFT)
    o_rw_tm, s_new_bl = _rwkv_sample(layer, tm(z3), st_shift, wl, state_bl)
    o_rw = tm(o_rw_tm).reshape(n_seq * t_new, BRANCH_W)
    s_new = jnp.transpose(s_new_bl, (3, 0, 1, 2))
    past = page_table.shape[1] * PAGE
    o_pool_tm, pool_buf_tm = _pool_sample(tm(st_pool), tm(u.reshape(n_seq, t_new, BRANCH_W)),
                                          wl["pool_w"], wl["pool_scale"], past)
    o_pool = tm(o_pool_tm).reshape(n_seq * t_new, BRANCH_W)
    cache_k, cache_v, cache_logf_t = caches
    tri = lambda a, w: a.reshape(n_seq, t_new, w)
    lf3 = lf[:, :HEADS].reshape(n_seq, t_new, HEADS)
    o_fox3 = _fox_sample(layer, page_table, tri(qb, BRANCH_W), tri(k, BRANCH_W), tri(v, BRANCH_W),
                         jnp.transpose(lf3, (0, 2, 1)), cache_k, cache_v, cache_logf_t)
    y = _outproj(x2d, (o_rw, o_pool, o_fox3.reshape(n_seq * t_new, BRANCH_W)), (grw, gpool, gfox), gm,
                 wl, 256)
    state = (s_new, z3[:, -1], tm(pool_buf_tm),
             k.reshape(n_seq, t_new, HEADS, HEAD_DIM), v.reshape(n_seq, t_new, HEADS, HEAD_DIM), lf3)
    return y, state


def kernel(x_prompt, x_sample, state_rwkv, state_shift, state_pool, cache_k, cache_v, cache_logf, page_table, norm_pre, norm_post, w_in, b_forget, rw_mu, rw_w0, rw_w2, rw_a0, rw_a2, rw_kk, rw_ka, rw_rk, rw_lnw, rw_lnb, pool_w, pool_scale, proj_a, proj_b, proj_c, w_out):
    depth = w_in.shape[0]
    bp, sp, _ = x_prompt.shape
    bs, ts, _ = x_sample.shape
    params = dict(norm_pre=norm_pre, norm_post=norm_post, b_forget=b_forget, rw_mu=rw_mu, rw_w0=rw_w0,
                  rw_w2=rw_w2, rw_a0=rw_a0, rw_a2=rw_a2, rw_kk=rw_kk, rw_ka=rw_ka, rw_rk=rw_rk,
                  rw_lnw=rw_lnw, rw_lnb=rw_lnb, pool_w=pool_w, pool_scale=pool_scale, proj_a=proj_a,
                  proj_b=proj_b, proj_c=proj_c, w_out=w_out)
    w_cat = _reorder_w_in(w_in)
    n_pool = cache_k.shape[1]
    keys_minor = lambda c: jnp.transpose(c, (0, 1, 3, 4, 2)).reshape(depth, n_pool, BRANCH_W, PAGE)
    caches = (keys_minor(cache_k), keys_minor(cache_v), jnp.transpose(cache_logf, (0, 1, 3, 2)))
    state_bl = jnp.transpose(state_rwkv, (0, 2, 3, 4, 1))
    hp = x_prompt.reshape(bp * sp, D_MODEL)
    hs = x_sample.reshape(bs * ts, D_MODEL)
    kv_bufs = (jnp.zeros((depth, bp, BRANCH_W, sp), F32), jnp.zeros((depth, bp, BRANCH_W, sp), F32))
    st_p, st_s = [], []
    for l in range(depth):
        wl = _layer_weights(l, w_cat, params)
        hp, sp_state, kv_bufs = _prompt_layer(hp, wl, l, kv_bufs, bp, sp)
        hs, ss_state = _sample_layer(hs, wl, l, bs, ts, state_bl, state_shift[l], state_pool[l],
                                     caches, page_table)
        st_p.append(sp_state)
        st_s.append(ss_state)
    stack = lambda states, i: jnp.stack([s[i] for s in states])
    rows_major = lambda buf: jnp.transpose(buf.reshape(depth, bp, HEADS, HEAD_DIM, sp), (0, 1, 4, 2, 3))
    return (hp.reshape(bp, sp, D_MODEL), hs.reshape(bs, ts, D_MODEL),
            stack(st_p, 0), stack(st_s, 0), stack(st_p, 1), stack(st_s, 1), stack(st_p, 2), stack(st_s, 2),
            rows_major(kv_bufs[0]), stack(st_s, 3), rows_major(kv_bufs[1]), stack(st_s, 4),
            stack(st_p, 3), stack(st_s, 5))
```
